```python
import math
import jax, jax.numpy as jnp
from jax import lax
import numpy as np

D_MODEL = 4096
BATCH = 8
SEQ = 2048
DEPTH = 2

LRU_WIDTH = D_MODEL
LRU_HEADS = 16
LRU_HEAD_DIM = LRU_WIDTH // LRU_HEADS
CONV_WIDTH = 4
RG_C = 8.0
POOL_WINDOWS = (2, 4, 8, 16)
POOL_GROUPS = len(POOL_WINDOWS)
POOL_WIDTH = D_MODEL
POOL_GROUP_DIM = POOL_WIDTH // POOL_GROUPS
IN_WIDTH = 2 * LRU_WIDTH + POOL_WIDTH + 2 * D_MODEL
PEER_HEADS = 8
PEER_QUERY_DIM = 256
PEER_HALF = PEER_QUERY_DIM // 2
N_KEYS = 128
N_EXPERTS = N_KEYS * N_KEYS
PEER_TOPK = 16
PEER_CHUNK = 64
RMS_EPS = 1e-6

kernel_name = "hybrid_rglru_pool_peer"


def rms_norm(x, g):
    xf = x.astype(jnp.float32)
    y = xf * lax.rsqrt(jnp.mean(xf * xf, axis=-1, keepdims=True) + RMS_EPS)
    return (y * g.astype(jnp.float32)).astype(x.dtype)


def causal_depthwise_conv(x, w, b):
    S = x.shape[1]
    xp = jnp.pad(x, ((0, 0), (CONV_WIDTH - 1, 0), (0, 0)))
    y = sum(xp[:, k:k + S] * w[k] for k in range(CONV_WIDTH))
    return y + b


def rg_lru(x, w_rg, b_rg, w_ig, b_ig, lam):
    B, S, C = x.shape
    xh = x.reshape(B, S, LRU_HEADS, LRU_HEAD_DIM)
    r = jax.nn.sigmoid(jnp.einsum('bshi,hij->bshj', xh, w_rg).reshape(B, S, C) + b_rg)
    i = jax.nn.sigmoid(jnp.einsum('bshi,hij->bshj', xh, w_ig).reshape(B, S, C) + b_ig)
    log_a = -RG_C * r.astype(jnp.float32) * jax.nn.softplus(-lam.astype(jnp.float32))
    a = jnp.exp(log_a)
    mult = jnp.sqrt(-jnp.expm1(2.0 * log_a))
    u = mult * (i * x).astype(jnp.float32)

    def combine(left, right):
        a1, b1 = left
        a2, b2 = right
        return a1 * a2, a2 * b1 + b2

    _, h = lax.associative_scan(combine, (a, u), axis=1)
    return h.astype(x.dtype)


def pool_mixer(xp, w_pool, scale):
    B, S, C = xp.shape
    xf = xp.astype(jnp.float32)
    cs_pad = jnp.concatenate([jnp.zeros((B, 1, C), jnp.float32), jnp.cumsum(xf, axis=1)], axis=1)
    pos = jnp.arange(1, S + 1, dtype=jnp.float32)[None, :, None]
    groups = []
    for g, w in enumerate(POOL_WINDOWS):
        sl = slice(g * POOL_GROUP_DIM, (g + 1) * POOL_GROUP_DIM)
        c = cs_pad[:, :, sl]
        lag = jnp.concatenate([jnp.zeros((B, w - 1, POOL_GROUP_DIM), jnp.float32), c[:, :S + 1 - w]], axis=1)
        mean = (c[:, 1:] - lag) / jnp.minimum(pos, float(w))
        groups.append(mean - xf[:, :, sl])
    pooled = jnp.stack(groups, axis=2).astype(xp.dtype)
    y = jnp.einsum('bsgi,gij->bsgj', pooled, w_pool).reshape(B, S, C)
    return y * scale


def hybrid_mixer(xn, w_in, conv_w, conv_b, w_rg, b_rg, w_ig, b_ig, lru_lambda, w_pool, pool_scale, w_out):
    z = jnp.einsum('bsd,de->bse', xn, w_in)
    cuts = np.cumsum([LRU_WIDTH, LRU_WIDTH, POOL_WIDTH, D_MODEL])
    x_lru, g_gelu, x_pool, g_lru, g_pool = jnp.split(z, cuts, axis=-1)
    h = causal_depthwise_conv(x_lru, conv_w, conv_b)
    y_lru = rg_lru(h, w_rg, b_rg, w_ig, b_ig, lru_lambda) * jax.nn.gelu(g_gelu)
    y_pool = pool_mixer(x_pool, w_pool, pool_scale)
    merged = jax.nn.sigmoid(g_lru) * y_lru + jax.nn.sigmoid(g_pool) * y_pool
    return jnp.einsum('bsd,de->bse', merged, w_out)


def peer_ffn(xn, w_query, sub_keys, expert_u, expert_v):
    B, S, D = xn.shape
    T = B * S
    xt = xn.reshape(T, D)
    q = jnp.einsum('td,de->te', xt, w_query).reshape(T, PEER_HEADS, 2, PEER_HALF)
    s1 = jnp.einsum('thk,hnk->thn', q[:, :, 0], sub_keys[:, 0])
    s2 = jnp.einsum('thk,hnk->thn', q[:, :, 1], sub_keys[:, 1])
    v1, i1 = lax.top_k(s1, PEER_TOPK)
    v2, i2 = lax.top_k(s2, PEER_TOPK)
    cand = (v1[..., :, None] + v2[..., None, :]).reshape(T, PEER_HEADS, PEER_TOPK * PEER_TOPK)
    vals, pos = lax.top_k(cand, PEER_TOPK)
    e1 = jnp.take_along_axis(i1, pos // PEER_TOPK, axis=-1)
    e2 = jnp.take_along_axis(i2, pos % PEER_TOPK, axis=-1)
    experts = (e1 * N_KEYS + e2).reshape(T, PEER_HEADS * PEER_TOPK)
    gates = jax.nn.softmax(vals.astype(jnp.float32), axis=-1).astype(xn.dtype).reshape(T, PEER_HEADS * PEER_TOPK)

    n_chunks = T // PEER_CHUNK

    def expert_block(args):
        xc, ic, gc = args
        u = expert_u[ic]
        act = jax.nn.gelu(jnp.einsum('cd,ckd->ck', xc, u)) * gc
        return jnp.einsum('ck,ckd->cd', act, expert_v[ic])

    out = lax.map(expert_block, (xt.reshape(n_chunks, PEER_CHUNK, D),
                                 experts.reshape(n_chunks, PEER_CHUNK, -1),
                                 gates.reshape(n_chunks, PEER_CHUNK, -1)))
    return out.reshape(B, S, D)


def setup_inputs(seed: int = 0) -> dict:
    key = jax.random.key(seed)
    ks = jax.random.split(key, 20)
    f32 = jnp.float32
    L, D = DEPTH, D_MODEL

    def nrm(k, shape, scale):
        return jax.random.normal(k, shape, f32) * scale

    p = jax.random.uniform(ks[9], (L, LRU_WIDTH), f32, 0.9, 0.999) ** (1.0 / RG_C)
    return {
        "x": nrm(ks[0], (BATCH, SEQ, D), 1.0),
        "mix_norm": 1.0 + nrm(ks[1], (L, D), 0.02),
        "w_in": nrm(ks[2], (L, D, IN_WIDTH), D ** -0.5),
        "conv_w": nrm(ks[3], (L, CONV_WIDTH, LRU_WIDTH), CONV_WIDTH ** -0.5),
        "conv_b": nrm(ks[4], (L, LRU_WIDTH), 0.01),
        "w_rg": nrm(ks[5], (L, LRU_HEADS, LRU_HEAD_DIM, LRU_HEAD_DIM), LRU_HEAD_DIM ** -0.5),
        "b_rg": nrm(ks[6], (L, LRU_WIDTH), 0.01),
        "w_ig": nrm(ks[7], (L, LRU_HEADS, LRU_HEAD_DIM, LRU_HEAD_DIM), LRU_HEAD_DIM ** -0.5),
        "b_ig": nrm(ks[8], (L, LRU_WIDTH), 0.01),
        "lru_lambda": jnp.log(p) - jnp.log1p(-p),
        "w_pool": nrm(ks[10], (L, POOL_GROUPS, POOL_GROUP_DIM, POOL_GROUP_DIM), POOL_GROUP_DIM ** -0.5),
        "pool_scale": 1.0 + nrm(ks[11], (L, POOL_WIDTH), 0.02),
        "w_out": nrm(ks[12], (L, D, D), D ** -0.5),
        "ffn_norm": 1.0 + nrm(ks[13], (L, D), 0.02),
        "w_query": nrm(ks[14], (L, D, PEER_HEADS * PEER_QUERY_DIM), D ** -0.5),
        "sub_keys": nrm(ks[15], (L, PEER_HEADS, 2, N_KEYS, PEER_HALF), PEER_HALF ** -0.5),
        "expert_u": nrm(ks[16], (L, N_EXPERTS, D), D ** -0.5),
        "expert_v": nrm(ks[17], (L, N_EXPERTS, D), PEER_HEADS ** -0.5),
        "final_norm": 1.0 + nrm(ks[18], (D,), 0.02),
    }


def reference(x, mix_norm, w_in, conv_w, conv_b, w_rg, b_rg, w_ig, b_ig, lru_lambda, w_pool, pool_scale, w_out, ffn_norm, w_query, sub_keys, expert_u, expert_v, final_norm):
    h = x
    for l in range(DEPTH):
        xn = rms_norm(h, mix_norm[l])
        h = h + hybrid_mixer(xn, w_in[l], conv_w[l], conv_b[l], w_rg[l], b_rg[l], w_ig[l], b_ig[l],
                             lru_lambda[l], w_pool[l], pool_scale[l], w_out[l])
        xn = rms_norm(h, ffn_norm[l])
        h = h + peer_ffn(xn, w_query[l], sub_keys[l], expert_u[l], expert_v[l])
    return rms_norm(h, final_norm)
```

```python
import functools

import jax
import jax.numpy as jnp
from jax import lax
from jax.experimental import pallas as pl
from jax.experimental.pallas import tpu as pltpu

RMS_EPS = 1e-6
RG_C = 8.0
CONV_WIDTH = 4
POOL_WINDOWS = (2, 4, 8, 16)
PEER_TOPK = 16

LANES = 128
VMEM_LIMIT_BYTES = 56 * 1024 * 1024
HALO_ROWS_POOL = 16
HALO_ROWS_CONV = 8
POOL_TIME_BLOCK = 256
LRU_TIME_BLOCK = 512

_f32 = jnp.float32
_bf16 = jnp.bfloat16


def _params(*sem):
    return pltpu.CompilerParams(dimension_semantics=sem, vmem_limit_bytes=VMEM_LIMIT_BYTES)


def _blk(dim, pref):
    return pref if dim % pref == 0 else dim


def _rmsnorm_kernel(x_ref, g_ref, o_ref):
    x = x_ref[...]
    ms = jnp.mean(x * x, axis=-1, keepdims=True)
    o_ref[...] = (x * lax.rsqrt(ms + RMS_EPS) * g_ref[...]).astype(o_ref.dtype)


def _rmsnorm(x, g, out_dtype):
    t, d = x.shape
    tm = _blk(t, 256)
    return pl.pallas_call(
        _rmsnorm_kernel,
        grid=(t // tm,),
        in_specs=[pl.BlockSpec((tm, d), lambda i: (i, 0)),
                  pl.BlockSpec((1, d), lambda i: (0, 0))],
        out_specs=pl.BlockSpec((tm, d), lambda i: (i, 0)),
        out_shape=jax.ShapeDtypeStruct((t, d), out_dtype),
        compiler_params=_params("parallel"),
        name="rmsnorm",
    )(x, g.reshape(1, d))


def _matmul_kernel(a_ref, w_ref, o_ref):
    o_ref[...] = jnp.dot(a_ref[...], w_ref[...], preferred_element_type=_f32).astype(o_ref.dtype)


def _matmul_res_kernel(a_ref, w_ref, r_ref, o_ref):
    o_ref[...] = r_ref[...] + jnp.dot(a_ref[...], w_ref[...], preferred_element_type=_f32)


def _matmul(a, w, residual=None, out_dtype=_f32, name="matmul"):
    m, k = a.shape
    n = w.shape[1]
    tm = _blk(m, 1024)
    tn = _blk(n, 512)
    in_specs = [pl.BlockSpec((tm, k), lambda i, j: (i, 0)),
                pl.BlockSpec((k, tn), lambda i, j: (0, j))]
    args = [a, w]
    kern = _matmul_kernel
    if residual is not None:
        in_specs.append(pl.BlockSpec((tm, tn), lambda i, j: (i, j)))
        args.append(residual)
        kern = _matmul_res_kernel
    return pl.pallas_call(
        kern,
        grid=(m // tm, n // tn),
        in_specs=in_specs,
        out_specs=pl.BlockSpec((tm, tn), lambda i, j: (i, j)),
        out_shape=jax.ShapeDtypeStruct((m, n), out_dtype),
        compiler_params=_params("parallel", "arbitrary"),
        name=name,
    )(*args)


def _pool_kernel(x_ref, halo_ref, gate_ref, w_ref, scale_ref, o_ref, *, group_dim):
    t = pl.program_id(1)
    ts = x_ref.shape[0]
    row = lax.broadcasted_iota(jnp.int32, (ts, 1), 0) + t * ts
    first = t == 0
    for g, win in enumerate(POOL_WINDOWS):
        cols = slice(g * group_dim, (g + 1) * group_dim)
        x = x_ref[:, cols]
        halo = jnp.where(first, 0.0, halo_ref[:, cols])
        acc = jnp.concatenate([halo, x], axis=0)
        step = 1
        while step < win:
            acc = acc + pltpu.roll(acc, step, axis=0)
            step *= 2
        wsum = acc[HALO_ROWS_POOL:]
        count = jnp.minimum(row + 1, win).astype(_f32)
        pooled = wsum / count - x
        y = jnp.dot(pooled.astype(_bf16), w_ref[g], preferred_element_type=_f32)
        y = y * scale_ref[:, cols]
        o_ref[:, cols] = (jax.nn.sigmoid(gate_ref[:, cols]) * y).astype(o_ref.dtype)


def _pool_branch(z, w_pool, pool_scale, batch, seq, d):
    t = batch * seq
    groups, group_dim = w_pool.shape[0], w_pool.shape[1]
    ts = _blk(seq, POOL_TIME_BLOCK)
    nt = seq // ts
    hb = ts // HALO_ROWS_POOL
    x_col, gate_col = 2, 4
    return pl.pallas_call(
        functools.partial(_pool_kernel, group_dim=group_dim),
        grid=(batch, nt),
        in_specs=[
            pl.BlockSpec((ts, d), lambda b, s: (b * nt + s, x_col)),
            pl.BlockSpec((HALO_ROWS_POOL, d),
                         lambda b, s: (jnp.maximum((b * nt + s) * hb - 1, 0), x_col)),
            pl.BlockSpec((ts, d), lambda b, s: (b * nt + s, gate_col)),
            pl.BlockSpec((groups, group_dim, group_dim), lambda b, s: (0, 0, 0)),
            pl.BlockSpec((1, d), lambda b, s: (0, 0)),
        ],
        out_specs=pl.BlockSpec((ts, d), lambda b, s: (b * nt + s, 0)),
        out_shape=jax.ShapeDtypeStruct((t, d), _bf16),
        compiler_params=_params("parallel", "arbitrary"),
        name="pool_branch",
    )(z, z, z, w_pool, pool_scale.reshape(1, d))


def _lru_kernel(x_ref, halo_ref, gelu_ref, glru_ref, ypool_ref, convw_ref, convb_ref,
                wrg_ref, brg_ref, wig_ref, big_ref, lam_ref, o_ref, carry_ref):
    t = pl.program_id(2)
    ts = x_ref.shape[0]

    @pl.when(t == 0)
    def _():
        carry_ref[...] = jnp.zeros_like(carry_ref)

    x = x_ref[...]
    halo = jnp.where(t == 0, 0.0, halo_ref[...])
    ext = jnp.concatenate([halo, x], axis=0)
    conv = convb_ref[...] + x * convw_ref[CONV_WIDTH - 1:CONV_WIDTH, :]
    for k in range(CONV_WIDTH - 1):
        shifted = pltpu.roll(ext, CONV_WIDTH - 1 - k, axis=0)[HALO_ROWS_CONV:]
        conv = conv + shifted * convw_ref[k:k + 1, :]

    cb = conv.astype(_bf16)
    r = jax.nn.sigmoid(jnp.dot(cb, wrg_ref[0], preferred_element_type=_f32) + brg_ref[...])
    i = jax.nn.sigmoid(jnp.dot(cb, wig_ref[0], preferred_element_type=_f32) + big_ref[...])
    neg_lam = -lam_ref[...]
    softplus = jnp.maximum(neg_lam, 0.0) + jnp.log(1.0 + jnp.exp(-jnp.abs(neg_lam)))
    log_a = -RG_C * r * softplus
    a = jnp.exp(log_a)
    u = jnp.sqrt(1.0 - a * a) * (i * conv)

    rows = lax.broadcasted_iota(jnp.int32, (ts, 1), 0)
    step = 1
    while step < ts:
        keep = rows >= step
        a_prev = jnp.where(keep, pltpu.roll(a, step, axis=0), 1.0)
        u_prev = jnp.where(keep, pltpu.roll(u, step, axis=0), 0.0)
        u = a * u_prev + u
        a = a * a_prev
        step *= 2
    h = a * carry_ref[...] + u
    carry_ref[...] = h[ts - 1:ts, :]

    y = h * jax.nn.gelu(gelu_ref[...])
    merged = jax.nn.sigmoid(glru_ref[...]) * y + ypool_ref[...].astype(_f32)
    o_ref[...] = merged.astype(o_ref.dtype)


def _lru_branch(z, ypool, conv_w, conv_b, w_rg, b_rg, w_ig, b_ig, lam, batch, seq, d):
    t = batch * seq
    heads, hd = w_rg.shape[0], w_rg.shape[1]
    ts = _blk(seq, LRU_TIME_BLOCK)
    nt = seq // ts
    hb = ts // HALO_ROWS_CONV
    nh = d // hd
    row = lambda b, h, s: b * nt + s
    vec = pl.BlockSpec((1, hd), lambda b, h, s: (0, h))
    return pl.pallas_call(
        _lru_kernel,
        grid=(batch, heads, nt),
        in_specs=[
            pl.BlockSpec((ts, hd), lambda b, h, s: (row(b, h, s), h)),
            pl.BlockSpec((HALO_ROWS_CONV, hd),
                         lambda b, h, s: (jnp.maximum(row(b, h, s) * hb - 1, 0), h)),
            pl.BlockSpec((ts, hd), lambda b, h, s: (row(b, h, s), nh + h)),
            pl.BlockSpec((ts, hd), lambda b, h, s: (row(b, h, s), 3 * nh + h)),
            pl.BlockSpec((ts, hd), lambda b, h, s: (row(b, h, s), h)),
            pl.BlockSpec((CONV_WIDTH, hd), lambda b, h, s: (0, h)),
            vec,
            pl.BlockSpec((1, hd, hd), lambda b, h, s: (h, 0, 0)),
            vec,
            pl.BlockSpec((1, hd, hd), lambda b, h, s: (h, 0, 0)),
            vec,
            vec,
        ],
        out_specs=pl.BlockSpec((ts, hd), lambda b, h, s: (row(b, h, s), h)),
        out_shape=jax.ShapeDtypeStruct((t, d), _bf16),
        scratch_shapes=[pltpu.VMEM((1, hd), _f32)],
        compiler_params=_params("parallel", "parallel", "arbitrary"),
        name="lru_branch",
    )(z, z, z, z, ypool, conv_w, conv_b.reshape(1, d), w_rg, b_rg.reshape(1, d),
      w_ig, b_ig.reshape(1, d), lam.reshape(1, d))


def _topk_rows(s, k):
    nrows = s.shape[0]
    rows = lax.broadcasted_iota(jnp.int32, s.shape, 0).astype(_f32)
    vals, idxs = [], []
    for _ in range(k):
        m = jnp.max(s, axis=0, keepdims=True)
        idx = jnp.min(jnp.where(s == m, rows, float(nrows)), axis=0, keepdims=True)
        vals.append(m)
        idxs.append(idx)
        s = jnp.where(rows == idx, -jnp.inf, s)
    return jnp.concatenate(vals, axis=0), jnp.concatenate(idxs, axis=0)


def _route_kernel(x_ref, wq_ref, keys_ref, idx_ref, gate_ref, *, n_keys):
    half = keys_ref.shape[-1]
    q = jnp.dot(x_ref[...], wq_ref[...], preferred_element_type=_f32).astype(_bf16)
    nt = (((1,), (1,)), ((), ()))
    s1 = lax.dot_general(keys_ref[0, 0], q[:, :half], nt, preferred_element_type=_f32)
    s2 = lax.dot_general(keys_ref[0, 1], q[:, half:], nt, preferred_element_type=_f32)
    v1, i1 = _topk_rows(s1, PEER_TOPK)
    v2, i2 = _topk_rows(s2, PEER_TOPK)
    cand = jnp.concatenate([v1[a:a + 1] + v2 for a in range(PEER_TOPK)], axis=0)
    vals, pos = _topk_rows(cand, PEER_TOPK)
    pos_hi = jnp.floor(pos * (1.0 / PEER_TOPK))
    pos_lo = pos - pos_hi * PEER_TOPK
    e1 = jnp.zeros_like(pos)
    e2 = jnp.zeros_like(pos)
    for a in range(PEER_TOPK):
        e1 = e1 + jnp.where(pos_hi == a, i1[a:a + 1], 0.0)
        e2 = e2 + jnp.where(pos_lo == a, i2[a:a + 1], 0.0)
    idx_ref[0] = (e1 * n_keys + e2).astype(jnp.int32)
    ex = jnp.exp(vals - vals[0:1])
    gate_ref[0] = ex / jnp.sum(ex, axis=0, keepdims=True)


def _route(xn, w_query, sub_keys):
    t, d = xn.shape
    heads, _, n_keys, half = sub_keys.shape
    qd = 2 * half
    tm = _blk(t, 256)
    out_sds = lambda dt: jax.ShapeDtypeStruct((heads, PEER_TOPK, t), dt)
    idx, gates = pl.pallas_call(
        functools.partial(_route_kernel, n_keys=n_keys),
        grid=(t // tm, heads),
        in_specs=[pl.BlockSpec((tm, d), lambda i, h: (i, 0)),
                  pl.BlockSpec((d, qd), lambda i, h: (0, h)),
                  pl.BlockSpec((1, 2, n_keys, half), lambda i, h: (h, 0, 0, 0))],
        out_specs=[pl.BlockSpec((1, PEER_TOPK, tm), lambda i, h: (h, 0, i)),
                   pl.BlockSpec((1, PEER_TOPK, tm), lambda i, h: (h, 0, i))],
        out_shape=[out_sds(jnp.int32), out_sds(_f32)],
        compiler_params=_params("parallel", "arbitrary"),
        name="peer_route",
    )(xn, w_query, sub_keys)
    flat = lambda a: jnp.transpose(a, (2, 0, 1)).reshape(t, heads * PEER_TOPK)
    return flat(idx), flat(gates)


PEER_TOKENS_PER_STEP = 128
PEER_TOKENS_PER_GROUP = 4


def _peer_kernel(idx_ref, gate_ref, x_ref, h_ref, tab_ref, o_ref, buf_ref, sem_ref, *, n_sel):
    tb, rr, _ = x_ref.shape
    grp = PEER_TOKENS_PER_GROUP
    n_groups = tb // grp
    rows_per_group = grp * n_sel
    dd = rr * LANES

    def group_copy(slot):
        return pltpu.make_async_copy(tab_ref.at[pl.ds(0, rows_per_group)], buf_ref.at[slot],
                                     sem_ref.at[slot])

    def issue(g, slot):
        def body(j, carry):
            tok = g * grp + j // n_sel
            e = idx_ref[tok, j % n_sel]
            pltpu.make_async_copy(tab_ref.at[e], buf_ref.at[slot, j], sem_ref.at[slot]).start()
            return carry
        lax.fori_loop(0, rows_per_group, body, 0)

    lane_r = lax.broadcasted_iota(jnp.int32, (rr, n_sel * rr), 1) % rr
    diag = lane_r == lax.broadcasted_iota(jnp.int32, (rr, n_sel * rr), 0)
    col_k = lax.broadcasted_iota(jnp.int32, (n_sel, n_sel * rr), 1) // rr
    expand = (col_k == lax.broadcasted_iota(jnp.int32, (n_sel, n_sel * rr), 0)).astype(_bf16)
    nt = (((1,), (1,)), ((), ()))

    issue(0, 0)

    def group_body(g, carry):
        slot = g % 2

        @pl.when(g + 1 < n_groups)
        def _():
            issue(g + 1, 1 - slot)

        group_copy(slot).wait()
        base = pl.multiple_of(g * grp, grp)
        partial = []
        for j in range(grp):
            u = buf_ref[slot, pl.ds(j * n_sel, n_sel), 0:rr, :].reshape(n_sel * rr, LANES)
            xs = x_ref[base + j]
            prod = lax.dot_general(xs, u, nt, preferred_element_type=_f32)
            partial.append(jnp.sum(jnp.where(diag, prod, 0.0), axis=0, keepdims=True))
        partial = jnp.concatenate(partial, axis=0)
        p_hi = partial.astype(_bf16)
        p_lo = (partial - p_hi.astype(_f32)).astype(_bf16)
        scores = (lax.dot_general(p_hi, expand, nt, preferred_element_type=_f32)
                  + lax.dot_general(p_lo, expand, nt, preferred_element_type=_f32))
        gate = jnp.concatenate([gate_ref[base + j] for j in range(grp)], axis=0)
        act = jax.nn.gelu(scores) * gate
        act_wide = jnp.dot(act.astype(_bf16), expand, preferred_element_type=_f32)
        for j in range(grp):
            v = buf_ref[slot, pl.ds(j * n_sel, n_sel), rr:2 * rr, :].reshape(n_sel * rr, LANES)
            lhs = jnp.where(diag, act_wide[j:j + 1, :], 0.0).astype(_bf16)
            out = jnp.dot(lhs, v, preferred_element_type=_f32)
            o_ref[base + j] = h_ref[base + j] + out
        return carry

    lax.fori_loop(0, n_groups, group_body, 0)


def _peer_experts(idx, gates, xn, h, table):
    t, d = xn.shape
    rr = d // LANES
    n_sel = idx.shape[1]
    tb = _blk(t, PEER_TOKENS_PER_STEP)
    rows_per_group = PEER_TOKENS_PER_GROUP * n_sel
    out = pl.pallas_call(
        functools.partial(_peer_kernel, n_sel=n_sel),
        grid=(t // tb,),
        in_specs=[
            pl.BlockSpec((tb, n_sel), lambda i: (i, 0), memory_space=pltpu.SMEM),
            pl.BlockSpec((tb, 1, n_sel), lambda i: (i, 0, 0)),
            pl.BlockSpec((tb, rr, LANES), lambda i: (i, 0, 0)),
            pl.BlockSpec((tb, rr, LANES), lambda i: (i, 0, 0)),
            pl.BlockSpec(memory_space=pl.ANY),
        ],
        out_specs=pl.BlockSpec((tb, rr, LANES), lambda i: (i, 0, 0)),
        out_shape=jax.ShapeDtypeStruct((t, rr, LANES), _f32),
        scratch_shapes=[pltpu.VMEM((2, rows_per_group, 2 * rr, LANES), _bf16),
                        pltpu.SemaphoreType.DMA((2,))],
        compiler_params=_params("arbitrary"),
        name="peer_experts",
    )(idx, gates.reshape(t, 1, n_sel), xn.reshape(t, rr, LANES), h.reshape(t, rr, LANES), table)
    return out.reshape(t, d)


def _expert_table(expert_u, expert_v):
    e, d = expert_u.shape
    rr = d // LANES
    return jnp.concatenate([expert_u.astype(_bf16).reshape(e, rr, LANES),
                            expert_v.astype(_bf16).reshape(e, rr, LANES)], axis=1)


def kernel(x, mix_norm, w_in, conv_w, conv_b, w_rg, b_rg, w_ig, b_ig, lru_lambda, w_pool, pool_scale,
           w_out, ffn_norm, w_query, sub_keys, expert_u, expert_v, final_norm):
    batch, seq, d = x.shape
    depth = w_in.shape[0]
    t = batch * seq
    h = x.reshape(t, d)
    for l in range(depth):
        xn = _rmsnorm(h, mix_norm[l], _bf16)
        z = _matmul(xn, w_in[l].astype(_bf16), name="in_proj")
        ypool = _pool_branch(z, w_pool[l].astype(_bf16), pool_scale[l], batch, seq, d)
        merged = _lru_branch(z, ypool, conv_w[l], conv_b[l], w_rg[l].astype(_bf16), b_rg[l],
                             w_ig[l].astype(_bf16), b_ig[l], lru_lambda[l], batch, seq, d)
        h = _matmul(merged, w_out[l].astype(_bf16), residual=h, name="out_proj")
        xn = _rmsnorm(h, ffn_norm[l], _bf16)
        idx, gates = _route(xn, w_query[l].astype(_bf16), sub_keys[l].astype(_bf16))
        h = _peer_experts(idx, gates, xn, h, _expert_table(expert_u[l], expert_v[l]))
    return _rmsnorm(h, final_norm, _f32).reshape(batch, seq, d)
```

```python
import functools

import jax
import jax.numpy as jnp
from jax import lax
from jax.experimental import pallas as pl
from jax.experimental.pallas import tpu as pltpu

RMS_EPS = 1e-6
RG_C = 8.0
CONV_WIDTH = 4
POOL_WINDOWS = (2, 4, 8, 16)
PEER_TOPK = 16

LANES = 128
VMEM_LIMIT_BYTES = 56 * 1024 * 1024
HALO_ROWS_POOL = 16
HALO_ROWS_CONV = 8
POOL_TIME_BLOCK = 256
LRU_TIME_BLOCK = 512

_f32 = jnp.float32
_bf16 = jnp.bfloat16


def _params(*sem):
    return pltpu.CompilerParams(dimension_semantics=sem, vmem_limit_bytes=VMEM_LIMIT_BYTES)


def _blk(dim, pref):
    return pref if dim % pref == 0 else dim


def _rmsnorm_kernel(x_ref, g_ref, o_ref):
    x = x_ref[...]
    ms = jnp.mean(x * x, axis=-1, keepdims=True)
    o_ref[...] = (x * lax.rsqrt(ms + RMS_EPS) * g_ref[...]).astype(o_ref.dtype)


def _rmsnorm(x, g, out_dtype):
    t, d = x.shape
    tm = _blk(t, 256)
    return pl.pallas_call(
        _rmsnorm_kernel,
        grid=(t // tm,),
        in_specs=[pl.BlockSpec((tm, d), lambda i: (i, 0)),
                  pl.BlockSpec((1, d), lambda i: (0, 0))],
        out_specs=pl.BlockSpec((tm, d), lambda i: (i, 0)),
        out_shape=jax.ShapeDtypeStruct((t, d), out_dtype),
        compiler_params=_params("parallel"),
        name="rmsnorm",
    )(x, g.reshape(1, d))


def _matmul_kernel(a_ref, w_ref, o_ref):
    o_ref[...] = jnp.dot(a_ref[...], w_ref[...], preferred_element_type=_f32).astype(o_ref.dtype)


def _matmul_res_kernel(a_ref, w_ref, r_ref, o_ref):
    o_ref[...] = r_ref[...] + jnp.dot(a_ref[...], w_ref[...], preferred_element_type=_f32)


def _matmul(a, w, residual=None, out_dtype=_f32, name="matmul"):
    m, k = a.shape
    n = w.shape[1]
    tm = _blk(m, 1024)
    tn = _blk(n, 512)
    in_specs = [pl.BlockSpec((tm, k), lambda i, j: (i, 0)),
                pl.BlockSpec((k, tn), lambda i, j: (0, j))]
    args = [a, w]
    kern = _matmul_kernel
    if residual is not None:
        in_specs.append(pl.BlockSpec((tm, tn), lambda i, j: (i, j)))
        args.append(residual)
        kern = _matmul_res_kernel
    return pl.pallas_call(
        kern,
        grid=(m // tm, n // tn),
        in_specs=in_specs,
        out_specs=pl.BlockSpec((tm, tn), lambda i, j: (i, j)),
        out_shape=jax.ShapeDtypeStruct((m, n), out_dtype),
        compiler_params=_params("parallel", "arbitrary"),
        name=name,
    )(*args)


def _pool_kernel(x_ref, halo_ref, gate_ref, w_ref, scale_ref, o_ref, *, group_dim):
    t = pl.program_id(1)
    ts = x_ref.shape[0]
    row = lax.broadcasted_iota(jnp.int32, (ts, 1), 0) + t * ts
    first = t == 0
    for g, win in enumerate(POOL_WINDOWS):
        cols = slice(g * group_dim, (g + 1) * group_dim)
        x = x_ref[:, cols]
        halo = jnp.where(first, 0.0, halo_ref[:, cols])
        acc = jnp.concatenate([halo, x], axis=0)
        step = 1
        while step < win:
            acc = acc + pltpu.roll(acc, step, axis=0)
            step *= 2
        wsum = acc[HALO_ROWS_POOL:]
        count = jnp.minimum(row + 1, win).astype(_f32)
        pooled = wsum / count - x
        y = jnp.dot(pooled.astype(_bf16), w_ref[g], preferred_element_type=_f32)
        y = y * scale_ref[:, cols]
        o_ref[:, cols] = (jax.nn.sigmoid(gate_ref[:, cols]) * y).astype(o_ref.dtype)


def _pool_branch(z, w_pool, pool_scale, batch, seq, d):
    t = batch * seq
    groups, group_dim = w_pool.shape[0], w_pool.shape[1]
    ts = _blk(seq, POOL_TIME_BLOCK)
    nt = seq // ts
    hb = ts // HALO_ROWS_POOL
    x_col, gate_col = 2, 4
    return pl.pallas_call(
        functools.partial(_pool_kernel, group_dim=group_dim),
        grid=(batch, nt),
        in_specs=[
            pl.BlockSpec((ts, d), lambda b, s: (b * nt + s, x_col)),
            pl.BlockSpec((HALO_ROWS_POOL, d),
                         lambda b, s: (jnp.maximum((b * nt + s) * hb - 1, 0), x_col)),
            pl.BlockSpec((ts, d), lambda b, s: (b * nt + s, gate_col)),
            pl.BlockSpec((groups, group_dim, group_dim), lambda b, s: (0, 0, 0)),
            pl.BlockSpec((1, d), lambda b, s: (0, 0)),
        ],
        out_specs=pl.BlockSpec((ts, d), lambda b, s: (b * nt + s, 0)),
        out_shape=jax.ShapeDtypeStruct((t, d), _bf16),
        compiler_params=_params("parallel", "arbitrary"),
        name="pool_branch",
    )(z, z, z, w_pool, pool_scale.reshape(1, d))


def _lru_kernel(x_ref, halo_ref, gelu_ref, glru_ref, ypool_ref, convw_ref, convb_ref,
                wrg_ref, brg_ref, wig_ref, big_ref, lam_ref, o_ref, carry_ref):
    t = pl.program_id(2)
    ts = x_ref.shape[0]

    @pl.when(t == 0)
    def _():
        carry_ref[...] = jnp.zeros_like(carry_ref)

    x = x_ref[...]
    halo = jnp.where(t == 0, 0.0, halo_ref[...])
    ext = jnp.concatenate([halo, x], axis=0)
    conv = convb_ref[...] + x * convw_ref[CONV_WIDTH - 1:CONV_WIDTH, :]
    for k in range(CONV_WIDTH - 1):
        shifted = pltpu.roll(ext, CONV_WIDTH - 1 - k, axis=0)[HALO_ROWS_CONV:]
        conv = conv + shifted * convw_ref[k:k + 1, :]

    cb = conv.astype(_bf16)
    r = jax.nn.sigmoid(jnp.dot(cb, wrg_ref[0], preferred_element_type=_f32) + brg_ref[...])
    i = jax.nn.sigmoid(jnp.dot(cb, wig_ref[0], preferred_element_type=_f32) + big_ref[...])
    neg_lam = -lam_ref[...]
    softplus = jnp.maximum(neg_lam, 0.0) + jnp.log(1.0 + jnp.exp(-jnp.abs(neg_lam)))
    log_a = -RG_C * r * softplus
    a = jnp.exp(log_a)
    u = jnp.sqrt(1.0 - a * a) * (i * conv)

    rows = lax.broadcasted_iota(jnp.int32, (ts, 1), 0)
    step = 1
    while step < ts:
        keep = rows >= step
        a_prev = jnp.where(keep, pltpu.roll(a, step, axis=0), 1.0)
        u_prev = jnp.where(keep, pltpu.roll(u, step, axis=0), 0.0)
        u = a * u_prev + u
        a = a * a_prev
        step *= 2
    h = a * carry_ref[...] + u
    carry_ref[...] = h[ts - 1:ts, :]

    y = h * jax.nn.gelu(gelu_ref[...])
    merged = jax.nn.sigmoid(glru_ref[...]) * y + ypool_ref[...].astype(_f32)
    o_ref[...] = merged.astype(o_ref.dtype)


def _lru_branch(z, ypool, conv_w, conv_b, w_rg, b_rg, w_ig, b_ig, lam, batch, seq, d):
    t = batch * seq
    heads, hd = w_rg.shape[0], w_rg.shape[1]
    ts = _blk(seq, LRU_TIME_BLOCK)
    nt = seq // ts
    hb = ts // HALO_ROWS_CONV
    nh = d // hd
    row = lambda b, h, s: b * nt + s
    vec = pl.BlockSpec((1, hd), lambda b, h, s: (0, h))
    return pl.pallas_call(
        _lru_kernel,
        grid=(batch, heads, nt),
        in_specs=[
            pl.BlockSpec((ts, hd), lambda b, h, s: (row(b, h, s), h)),
            pl.BlockSpec((HALO_ROWS_CONV, hd),
                         lambda b, h, s: (jnp.maximum(row(b, h, s) * hb - 1, 0), h)),
            pl.BlockSpec((ts, hd), lambda b, h, s: (row(b, h, s), nh + h)),
            pl.BlockSpec((ts, hd), lambda b, h, s: (row(b, h, s), 3 * nh + h)),
            pl.BlockSpec((ts, hd), lambda b, h, s: (row(b, h, s), h)),
            pl.BlockSpec((CONV_WIDTH, hd), lambda b, h, s: (0, h)),
            vec,
            pl.BlockSpec((1, hd, hd), lambda b, h, s: (h, 0, 0)),
            vec,
            pl.BlockSpec((1, hd, hd), lambda b, h, s: (h, 0, 0)),
            vec,
            vec,
        ],
        out_specs=pl.BlockSpec((ts, hd), lambda b, h, s: (row(b, h, s), h)),
        out_shape=jax.ShapeDtypeStruct((t, d), _bf16),
        scratch_shapes=[pltpu.VMEM((1, hd), _f32)],
        compiler_params=_params("parallel", "parallel", "arbitrary"),
        name="lru_branch",
    )(z, z, z, z, ypool, conv_w, conv_b.reshape(1, d), w_rg, b_rg.reshape(1, d),
      w_ig, b_ig.reshape(1, d), lam.reshape(1, d))


def _topk_rows(s, k, ids=None):
    if ids is None:
        ids = lax.broadcasted_iota(jnp.int32, s.shape, 0).astype(_f32)
    big = float(2 ** 20)
    vals, idxs = [], []
    for _ in range(k):
        m = jnp.max(s, axis=0, keepdims=True)
        idx = jnp.min(jnp.where(s == m, ids, big), axis=0, keepdims=True)
        vals.append(m)
        idxs.append(idx)
        s = jnp.where(ids == idx, -jnp.inf, s)
    return jnp.concatenate(vals, axis=0), jnp.concatenate(idxs, axis=0)


def _product_candidates(v1, v2):
    k = PEER_TOPK
    half = k // 2
    n = v1.shape[1]
    sub = lax.broadcasted_iota(jnp.int32, (half, n), 0)
    blocks = [v1[0:1] + v2]
    ids = [lax.broadcasted_iota(jnp.int32, (k, n), 0)]
    blocks.append(v1[1:2] + v2[0:half])
    ids.append(k + sub)
    blocks.append(v1[half:k] + v2[0:1])
    ids.append((sub + half) * k)
    for b in range(k // 3):
        ok = (sub >= 2) & ((sub + 1) * (b + 1) <= k)
        blocks.append(jnp.where(ok, v1[0:half] + v2[b:b + 1], -jnp.inf))
        ids.append(sub * k + b)
    return jnp.concatenate(blocks, axis=0), jnp.concatenate(ids, axis=0).astype(_f32)


def _route_kernel(x_ref, wq_ref, keys_ref, idx_ref, gate_ref, *, n_keys):
    half = keys_ref.shape[-1]
    q = jnp.dot(x_ref[...], wq_ref[...], preferred_element_type=_f32).astype(_bf16)
    nt = (((1,), (1,)), ((), ()))
    s1 = lax.dot_general(keys_ref[0, 0], q[:, :half], nt, preferred_element_type=_f32)
    s2 = lax.dot_general(keys_ref[0, 1], q[:, half:], nt, preferred_element_type=_f32)
    v1, i1 = _topk_rows(s1, PEER_TOPK)
    v2, i2 = _topk_rows(s2, PEER_TOPK)
    cand, cand_ids = _product_candidates(v1, v2)
    vals, pos = _topk_rows(cand, PEER_TOPK, cand_ids)
    pos_hi = jnp.floor(pos * (1.0 / PEER_TOPK))
    pos_lo = pos - pos_hi * PEER_TOPK
    e1 = jnp.zeros_like(pos)
    e2 = jnp.zeros_like(pos)
    for a in range(PEER_TOPK):
        e1 = e1 + jnp.where(pos_hi == a, i1[a:a + 1], 0.0)
        e2 = e2 + jnp.where(pos_lo == a, i2[a:a + 1], 0.0)
    idx_ref[0] = (e1 * n_keys + e2).astype(jnp.int32)
    ex = jnp.exp(vals - vals[0:1])
    gate_ref[0] = ex / jnp.sum(ex, axis=0, keepdims=True)


def _route(xn, w_query, sub_keys):
    t, d = xn.shape
    heads, _, n_keys, half = sub_keys.shape
    qd = 2 * half
    tm = _blk(t, 256)
    out_sds = lambda dt: jax.ShapeDtypeStruct((heads, PEER_TOPK, t), dt)
    idx, gates = pl.pallas_call(
        functools.partial(_route_kernel, n_keys=n_keys),
        grid=(t // tm, heads),
        in_specs=[pl.BlockSpec((tm, d), lambda i, h: (i, 0)),
                  pl.BlockSpec((d, qd), lambda i, h: (0, h)),
                  pl.BlockSpec((1, 2, n_keys, half), lambda i, h: (h, 0, 0, 0))],
        out_specs=[pl.BlockSpec((1, PEER_TOPK, tm), lambda i, h: (h, 0, i)),
                   pl.BlockSpec((1, PEER_TOPK, tm), lambda i, h: (h, 0, i))],
        out_shape=[out_sds(jnp.int32), out_sds(_f32)],
        compiler_params=_params("parallel", "arbitrary"),
        name="peer_route",
    )(xn, w_query, sub_keys)
    flat = lambda a: jnp.transpose(a, (2, 0, 1)).reshape(t, heads * PEER_TOPK)
    return flat(idx), flat(gates)


PEER_TOKENS_PER_STEP = 128
PEER_TOKENS_PER_GROUP = 2
PEER_RING = 4
PEER_ISSUE_UNROLL = 8
BF16_TILE_ROWS = 16


def _peer_kernel(idx_ref, gate_ref, x_ref, h_ref, tab_ref, o_ref, *scratch, n_sel):
    bufs, sem_ref = scratch[:PEER_RING], scratch[PEER_RING]
    tb, rows, _ = x_ref.shape
    rr = rows // 2
    grp = PEER_TOKENS_PER_GROUP
    n_groups = tb // grp
    width = n_sel * rr

    def widen(a):
        return jnp.concatenate([a[..., 0:rr, :], a[..., rr:2 * rr, :]], axis=-1)

    def slab_matrix(slot, j, first_row):
        part = lambda r0: bufs[slot][pl.ds(j * n_sel, n_sel), r0:r0 + rr, :].reshape(width, LANES)
        return jnp.concatenate([part(first_row), part(first_row + rr)], axis=-1)

    def start_copy(tok, k, slot, j):
        e = idx_ref[tok, k]
        pltpu.make_async_copy(tab_ref.at[e], bufs[slot].at[j * n_sel + k], sem_ref.at[slot]).start()

    def wait_group(slot):
        pltpu.make_async_copy(tab_ref.at[pl.ds(0, grp * n_sel)], bufs[slot], sem_ref.at[slot]).wait()

    lane_r = lax.broadcasted_iota(jnp.int32, (rr, width), 1) % rr
    diag = lane_r == lax.broadcasted_iota(jnp.int32, (rr, width), 0)
    col_k = lax.broadcasted_iota(jnp.int32, (n_sel, width), 1) // rr
    expand = (col_k == lax.broadcasted_iota(jnp.int32, (n_sel, width), 0)).astype(_bf16)
    nt = (((1,), (1,)), ((), ()))

    def group_step(g, slot, prefetch):
        wait_group(slot)
        base = g * grp
        ahead = (slot + PEER_RING - 1) % PEER_RING
        partial = []
        for j in range(grp):
            if prefetch:
                for k in range(n_sel):
                    start_copy(base + (PEER_RING - 1) * grp + j, k, ahead, j)
            u = slab_matrix(slot, j, 0)
            prod = lax.dot_general(widen(x_ref[base + j]), u, nt, preferred_element_type=_f32)
            partial.append(jnp.sum(jnp.where(diag, prod, 0.0), axis=0, keepdims=True))
        partial = jnp.concatenate(partial, axis=0)
        p_hi = partial.astype(_bf16)
        p_lo = (partial - p_hi.astype(_f32)).astype(_bf16)
        both = lax.dot_general(jnp.concatenate([p_hi, p_lo], axis=0), expand, nt,
                               preferred_element_type=_f32)
        scores = both[:grp] + both[grp:]
        gate = jnp.concatenate([gate_ref[base + j] for j in range(grp)], axis=0)
        act = jax.nn.gelu(scores) * gate
        act_wide = jnp.dot(act.astype(_bf16), expand, preferred_element_type=_f32)
        for j in range(grp):
            v = slab_matrix(slot, j, 2 * rr)
            lhs = jnp.where(diag, act_wide[j:j + 1, :], 0.0).astype(_bf16)
            out = jnp.dot(lhs, v, preferred_element_type=_f32)
            o_ref[base + j] = h_ref[base + j] + jnp.concatenate([out[:, :LANES], out[:, LANES:]], axis=0)

    for slot in range(PEER_RING - 1):
        for j in range(grp):
            def issue_chunk(c, carry, slot=slot, j=j):
                for kk in range(PEER_ISSUE_UNROLL):
                    start_copy(slot * grp + j, c * PEER_ISSUE_UNROLL + kk, slot, j)
                return carry
            lax.fori_loop(0, n_sel // PEER_ISSUE_UNROLL, issue_chunk, 0)

    def ring_body(it, carry):
        for slot in range(PEER_RING):
            group_step(it * PEER_RING + slot, slot, True)
        return carry

    lax.fori_loop(0, n_groups // PEER_RING - 1, ring_body, 0)
    for slot in range(PEER_RING):
        group_step(n_groups - PEER_RING + slot, slot, slot == 0)


def _peer_experts(idx, gates, xn, h, table):
    t, d = xn.shape
    rows = d // LANES
    n_sel = idx.shape[1]
    tb = _blk(t, PEER_TOKENS_PER_STEP)
    n_groups = tb // PEER_TOKENS_PER_GROUP
    assert n_groups % PEER_RING == 0 and n_groups >= 2 * PEER_RING
    assert n_sel % PEER_ISSUE_UNROLL == 0 and rows % (2 * BF16_TILE_ROWS) == 0
    slot_shape = (PEER_TOKENS_PER_GROUP * n_sel, 2 * rows, LANES)
    out = pl.pallas_call(
        functools.partial(_peer_kernel, n_sel=n_sel),
        grid=(t // tb,),
        in_specs=[
            pl.BlockSpec((tb, n_sel), lambda i: (i, 0), memory_space=pltpu.SMEM),
            pl.BlockSpec((tb, 1, n_sel), lambda i: (i, 0, 0)),
            pl.BlockSpec((tb, rows, LANES), lambda i: (i, 0, 0)),
            pl.BlockSpec((tb, rows, LANES), lambda i: (i, 0, 0)),
            pl.BlockSpec(memory_space=pl.ANY),
        ],
        out_specs=pl.BlockSpec((tb, rows, LANES), lambda i: (i, 0, 0)),
        out_shape=jax.ShapeDtypeStruct((t, rows, LANES), _f32),
        scratch_shapes=[pltpu.VMEM(slot_shape, _bf16) for _ in range(PEER_RING)]
                       + [pltpu.SemaphoreType.DMA((PEER_RING,))],
        compiler_params=_params("arbitrary"),
        name="peer_experts",
    )(idx, gates.reshape(t, 1, n_sel), xn.reshape(t, rows, LANES), h.reshape(t, rows, LANES), table)
    return out.reshape(t, d)


def _expert_table(expert_u, expert_v):
    e, d = expert_u.shape
    rows = d // LANES
    return jnp.concatenate([expert_u.astype(_bf16).reshape(e, rows, LANES),
                            expert_v.astype(_bf16).reshape(e, rows, LANES)], axis=1)


def kernel(x, mix_norm, w_in, conv_w, conv_b, w_rg, b_rg, w_ig, b_ig, lru_lambda, w_pool, pool_scale,
           w_out, ffn_norm, w_query, sub_keys, expert_u, expert_v, final_norm):
    batch, seq, d = x.shape
    depth = w_in.shape[0]
    t = batch * seq
    h = x.reshape(t, d)
    for l in range(depth):
        xn = _rmsnorm(h, mix_norm[l], _bf16)
        z = _matmul(xn, w_in[l].astype(_bf16), name="in_proj")
        ypool = _pool_branch(z, w_pool[l].astype(_bf16), pool_scale[l], batch, seq, d)
        merged = _lru_branch(z, ypool, conv_w[l], conv_b[l], w_rg[l].astype(_bf16), b_rg[l],
                             w_ig[l].astype(_bf16), b_ig[l], lru_lambda[l], batch, seq, d)
        h = _matmul(merged, w_out[l].astype(_bf16), residual=h, name="out_proj")
        xn = _rmsnorm(h, ffn_norm[l], _bf16)
        idx, gates = _route(xn, w_query[l].astype(_bf16), sub_keys[l].astype(_bf16))
        h = _peer_experts(idx, gates, xn, h, _expert_table(expert_u[l], expert_v[l]))
    return _rmsnorm(h, final_norm, _f32).reshape(batch, seq, d)
```

```python
import functools

import jax
import jax.numpy as jnp
from jax import lax
from jax.experimental import pallas as pl
from jax.experimental.pallas import tpu as pltpu

RMS_EPS = 1e-6
RG_C = 8.0
CONV_WIDTH = 4
POOL_WINDOWS = (2, 4, 8, 16)
PEER_TOPK = 16

LANES = 128
SUBLANES = 8
VMEM_LIMIT_BYTES = 56 * 1024 * 1024
HALO_ROWS_POOL = 16
HALO_ROWS_CONV = 8
POOL_TIME_BLOCK = 256
LRU_TIME_BLOCK = 512

_f32 = jnp.float32
_bf16 = jnp.bfloat16


def _params(*sem):
    return pltpu.CompilerParams(dimension_semantics=sem, vmem_limit_bytes=VMEM_LIMIT_BYTES)


def _blk(dim, pref):
    return pref if dim % pref == 0 else dim


def _rmsnorm_kernel(x_ref, g_ref, o_ref):
    x = x_ref[...]
    ms = jnp.mean(x * x, axis=-1, keepdims=True)
    o_ref[...] = (x * lax.rsqrt(ms + RMS_EPS) * g_ref[...]).astype(o_ref.dtype)


def _rmsnorm(x, g, out_dtype):
    t, d = x.shape
    tm = _blk(t, 256)
    return pl.pallas_call(
        _rmsnorm_kernel,
        grid=(t // tm,),
        in_specs=[pl.BlockSpec((tm, d), lambda i: (i, 0)),
                  pl.BlockSpec((1, d), lambda i: (0, 0))],
        out_specs=pl.BlockSpec((tm, d), lambda i: (i, 0)),
        out_shape=jax.ShapeDtypeStruct((t, d), out_dtype),
        compiler_params=_params("parallel"),
        name="rmsnorm",
    )(x, g.reshape(1, d))


def _matmul_kernel(a_ref, w_ref, o_ref):
    w = w_ref[...].astype(_bf16)
    o_ref[...] = jnp.dot(a_ref[...], w, preferred_element_type=_f32).astype(o_ref.dtype)


def _matmul_res_kernel(a_ref, w_ref, r_ref, o_ref):
    w = w_ref[...].astype(_bf16)
    o_ref[...] = r_ref[...] + jnp.dot(a_ref[...], w, preferred_element_type=_f32)


def _matmul(a, w_stack, layer, residual=None, out_dtype=_f32, name="matmul"):
    m, k = a.shape
    n = w_stack.shape[2]
    tm = _blk(m, 1024)
    tn = _blk(n, 512)
    in_specs = [pl.BlockSpec((tm, k), lambda i, j: (i, 0)),
                pl.BlockSpec((None, k, tn), lambda i, j: (layer, 0, j))]
    w = w_stack
    args = [a, w]
    kern = _matmul_kernel
    if residual is not None:
        in_specs.append(pl.BlockSpec((tm, tn), lambda i, j: (i, j)))
        args.append(residual)
        kern = _matmul_res_kernel
    return pl.pallas_call(
        kern,
        grid=(m // tm, n // tn),
        in_specs=in_specs,
        out_specs=pl.BlockSpec((tm, tn), lambda i, j: (i, j)),
        out_shape=jax.ShapeDtypeStruct((m, n), out_dtype),
        compiler_params=_params("parallel", "arbitrary"),
        name=name,
    )(*args)


def _pool_kernel(x_ref, halo_ref, gate_ref, w_ref, scale_ref, o_ref, *, group_dim):
    t = pl.program_id(1)
    ts = x_ref.shape[0]
    row = lax.broadcasted_iota(jnp.int32, (ts, 1), 0) + t * ts
    first = t == 0
    for g, win in enumerate(POOL_WINDOWS):
        cols = slice(g * group_dim, (g + 1) * group_dim)
        x = x_ref[:, cols]
        halo = jnp.where(first, 0.0, halo_ref[:, cols])
        acc = jnp.concatenate([halo, x], axis=0)
        step = 1
        while step < win:
            acc = acc + pltpu.roll(acc, step, axis=0)
            step *= 2
        wsum = acc[HALO_ROWS_POOL:]
        count = jnp.minimum(row + 1, win).astype(_f32)
        pooled = wsum / count - x
        y = jnp.dot(pooled.astype(_bf16), w_ref[g], preferred_element_type=_f32)
        y = y * scale_ref[:, cols]
        o_ref[:, cols] = (jax.nn.sigmoid(gate_ref[:, cols]) * y).astype(o_ref.dtype)


def _pool_branch(z, w_pool, pool_scale, batch, seq, d):
    t = batch * seq
    groups, group_dim = w_pool.shape[0], w_pool.shape[1]
    ts = _blk(seq, POOL_TIME_BLOCK)
    nt = seq // ts
    hb = ts // HALO_ROWS_POOL
    x_col, gate_col = 2, 4
    return pl.pallas_call(
        functools.partial(_pool_kernel, group_dim=group_dim),
        grid=(batch, nt),
        in_specs=[
            pl.BlockSpec((ts, d), lambda b, s: (b * nt + s, x_col)),
            pl.BlockSpec((HALO_ROWS_POOL, d),
                         lambda b, s: (jnp.maximum((b * nt + s) * hb - 1, 0), x_col)),
            pl.BlockSpec((ts, d), lambda b, s: (b * nt + s, gate_col)),
            pl.BlockSpec((groups, group_dim, group_dim), lambda b, s: (0, 0, 0)),
            pl.BlockSpec((1, d), lambda b, s: (0, 0)),
        ],
        out_specs=pl.BlockSpec((ts, d), lambda b, s: (b * nt + s, 0)),
        out_shape=jax.ShapeDtypeStruct((t, d), _bf16),
        compiler_params=_params("parallel", "arbitrary"),
        name="pool_branch",
    )(z, z, z, w_pool, pool_scale.reshape(1, d))


def _lru_kernel(x_ref, halo_ref, gelu_ref, glru_ref, ypool_ref, convw_ref, convb_ref,
                wrg_ref, brg_ref, wig_ref, big_ref, lam_ref, o_ref, carry_ref):
    t = pl.program_id(2)
    ts = x_ref.shape[0]

    @pl.when(t == 0)
    def _():
        carry_ref[...] = jnp.zeros_like(carry_ref)

    x = x_ref[...]
    halo = jnp.where(t == 0, 0.0, halo_ref[...])
    ext = jnp.concatenate([halo, x], axis=0)
    conv = convb_ref[...] + x * convw_ref[CONV_WIDTH - 1:CONV_WIDTH, :]
    for k in range(CONV_WIDTH - 1):
        shifted = pltpu.roll(ext, CONV_WIDTH - 1 - k, axis=0)[HALO_ROWS_CONV:]
        conv = conv + shifted * convw_ref[k:k + 1, :]

    cb = conv.astype(_bf16)
    r = jax.nn.sigmoid(jnp.dot(cb, wrg_ref[0], preferred_element_type=_f32) + brg_ref[...])
    i = jax.nn.sigmoid(jnp.dot(cb, wig_ref[0], preferred_element_type=_f32) + big_ref[...])
    neg_lam = -lam_ref[...]
    softplus = jnp.maximum(neg_lam, 0.0) + jnp.log(1.0 + jnp.exp(-jnp.abs(neg_lam)))
    log_a = -RG_C * r * softplus
    a = jnp.exp(log_a)
    u = jnp.sqrt(1.0 - a * a) * (i * conv)

    tiles = ts // SUBLANES
    a = a.reshape(tiles, SUBLANES, a.shape[-1])
    u = u.reshape(tiles, SUBLANES, u.shape[-1])
    sub = lax.broadcasted_iota(jnp.int32, a.shape, 1)
    step = 1
    while step < SUBLANES:
        keep = sub >= step
        a_prev = jnp.where(keep, pltpu.roll(a, step, axis=1), 1.0)
        u_prev = jnp.where(keep, pltpu.roll(u, step, axis=1), 0.0)
        u = a * u_prev + u
        a = a * a_prev
        step *= 2
    carry = carry_ref[...]
    hs = []
    for i in range(tiles):
        hs.append(a[i] * carry + u[i])
        carry = hs[-1][SUBLANES - 1:SUBLANES, :]
    carry_ref[...] = carry
    h = jnp.concatenate(hs, axis=0)

    y = h * jax.nn.gelu(gelu_ref[...])
    merged = jax.nn.sigmoid(glru_ref[...]) * y + ypool_ref[...].astype(_f32)
    o_ref[...] = merged.astype(o_ref.dtype)


def _lru_branch(z, ypool, conv_w, conv_b, w_rg, b_rg, w_ig, b_ig, lam, batch, seq, d):
    t = batch * seq
    heads, hd = w_rg.shape[0], w_rg.shape[1]
    ts = _blk(seq, LRU_TIME_BLOCK)
    nt = seq // ts
    hb = ts // HALO_ROWS_CONV
    nh = d // hd
    row = lambda b, h, s: b * nt + s
    vec = pl.BlockSpec((1, hd), lambda b, h, s: (0, h))
    return pl.pallas_call(
        _lru_kernel,
        grid=(batch, heads, nt),
        in_specs=[
            pl.BlockSpec((ts, hd), lambda b, h, s: (row(b, h, s), h)),
            pl.BlockSpec((HALO_ROWS_CONV, hd),
                         lambda b, h, s: (jnp.maximum(row(b, h, s) * hb - 1, 0), h)),
            pl.BlockSpec((ts, hd), lambda b, h, s: (row(b, h, s), nh + h)),
            pl.BlockSpec((ts, hd), lambda b, h, s: (row(b, h, s), 3 * nh + h)),
            pl.BlockSpec((ts, hd), lambda b, h, s: (row(b, h, s), h)),
            pl.BlockSpec((CONV_WIDTH, hd), lambda b, h, s: (0, h)),
            vec,
            pl.BlockSpec((1, hd, hd), lambda b, h, s: (h, 0, 0)),
            vec,
            pl.BlockSpec((1, hd, hd), lambda b, h, s: (h, 0, 0)),
            vec,
            vec,
        ],
        out_specs=pl.BlockSpec((ts, hd), lambda b, h, s: (row(b, h, s), h)),
        out_shape=jax.ShapeDtypeStruct((t, d), _bf16),
        scratch_shapes=[pltpu.VMEM((1, hd), _f32)],
        compiler_params=_params("parallel", "parallel", "arbitrary"),
        name="lru_branch",
    )(z, z, z, z, ypool, conv_w, conv_b.reshape(1, d), w_rg, b_rg.reshape(1, d),
      w_ig, b_ig.reshape(1, d), lam.reshape(1, d))


def _topk_rows(s, k, ids=None):
    if ids is None:
        ids = lax.broadcasted_iota(jnp.int32, s.shape, 0).astype(_f32)
    big = float(2 ** 20)
    vals, idxs = [], []
    for _ in range(k):
        m = jnp.max(s, axis=0, keepdims=True)
        idx = jnp.min(jnp.where(s == m, ids, big), axis=0, keepdims=True)
        vals.append(m)
        idxs.append(idx)
        s = jnp.where(ids == idx, -jnp.inf, s)
    return jnp.concatenate(vals, axis=0), jnp.concatenate(idxs, axis=0)


def _product_candidates(v1, v2):
    k = PEER_TOPK
    half = k // 2
    n = v1.shape[1]
    sub = lax.broadcasted_iota(jnp.int32, (half, n), 0)
    blocks = [v1[0:1] + v2]
    ids = [lax.broadcasted_iota(jnp.int32, (k, n), 0)]
    blocks.append(v1[1:2] + v2[0:half])
    ids.append(k + sub)
    blocks.append(v1[half:k] + v2[0:1])
    ids.append((sub + half) * k)
    for b in range(k // 3):
        ok = (sub >= 2) & ((sub + 1) * (b + 1) <= k)
        blocks.append(jnp.where(ok, v1[0:half] + v2[b:b + 1], -jnp.inf))
        ids.append(sub * k + b)
    return jnp.concatenate(blocks, axis=0), jnp.concatenate(ids, axis=0).astype(_f32)


def _route_kernel(x_ref, wq_ref, keys_ref, idx_ref, gate_ref, *, n_keys):
    half = keys_ref.shape[-1]
    q = jnp.dot(x_ref[...], wq_ref[...].astype(_bf16), preferred_element_type=_f32).astype(_bf16)
    nt = (((1,), (1,)), ((), ()))
    s1 = lax.dot_general(keys_ref[0, 0], q[:, :half], nt, preferred_element_type=_f32)
    s2 = lax.dot_general(keys_ref[0, 1], q[:, half:], nt, preferred_element_type=_f32)
    v1, i1 = _topk_rows(s1, PEER_TOPK)
    v2, i2 = _topk_rows(s2, PEER_TOPK)
    cand, cand_ids = _product_candidates(v1, v2)
    vals, pos = _topk_rows(cand, PEER_TOPK, cand_ids)
    pos_hi = jnp.floor(pos * (1.0 / PEER_TOPK))
    pos_lo = pos - pos_hi * PEER_TOPK
    e1 = jnp.zeros_like(pos)
    e2 = jnp.zeros_like(pos)
    for a in range(PEER_TOPK):
        e1 = e1 + jnp.where(pos_hi == a, i1[a:a + 1], 0.0)
        e2 = e2 + jnp.where(pos_lo == a, i2[a:a + 1], 0.0)
    idx_ref[0] = (e1 * n_keys + e2).astype(jnp.int32)
    ex = jnp.exp(vals - vals[0:1])
    gate_ref[0] = ex / jnp.sum(ex, axis=0, keepdims=True)


def _route(xn, w_query_stack, layer, sub_keys):
    t, d = xn.shape
    heads, _, n_keys, half = sub_keys.shape
    qd = 2 * half
    tm = _blk(t, 256)
    out_sds = lambda dt: jax.ShapeDtypeStruct((heads, PEER_TOPK, t), dt)
    idx, gates = pl.pallas_call(
        functools.partial(_route_kernel, n_keys=n_keys),
        grid=(t // tm, heads),
        in_specs=[pl.BlockSpec((tm, d), lambda i, h: (i, 0)),
                  pl.BlockSpec((None, d, qd), lambda i, h: (layer, 0, h)),
                  pl.BlockSpec((1, 2, n_keys, half), lambda i, h: (h, 0, 0, 0))],
        out_specs=[pl.BlockSpec((1, PEER_TOPK, tm), lambda i, h: (h, 0, i)),
                   pl.BlockSpec((1, PEER_TOPK, tm), lambda i, h: (h, 0, i))],
        out_shape=[out_sds(jnp.int32), out_sds(_f32)],
        compiler_params=_params("parallel", "arbitrary"),
        name="peer_route",
    )(xn, w_query_stack, sub_keys)
    flat =lambda a: jnp.transpose(a, (2, 0, 1)).reshape(t, heads * PEER_TOPK)
    return flat(idx), flat(gates)


PEER_TOKENS_PER_STEP = 128
PEER_TOKENS_PER_GROUP = 2
PEER_RING = 4
PEER_AHEAD = 2
PEER_ISSUE_UNROLL = 8
BF16_TILE_ROWS = 16


def _peer_kernel(idx_ref, idx_next_ref, gate_ref, x_ref, h_ref, g_ref, tab_ref, *rest, n_sel, emit_h):
    n_out = 2 if emit_h else 1
    outs, scratch = rest[:n_out], rest[n_out:]
    o_ref = outs[0] if emit_h else None
    n_ref = outs[-1]
    bufs, sem_ref = scratch[:PEER_RING], scratch[PEER_RING]
    tb, rows, _ = x_ref.shape
    rr = rows // 2
    grp = PEER_TOKENS_PER_GROUP
    n_groups = tb // grp
    width = n_sel * rr
    step = pl.program_id(0)
    last_step = pl.num_programs(0) - 1

    def widen(a):
        return jnp.concatenate([a[..., 0:rr, :], a[..., rr:2 * rr, :]], axis=-1)

    def slab_matrix(slot, j, first_row):
        part = lambda r0: bufs[slot][pl.ds(j * n_sel, n_sel), r0:r0 + rr, :].reshape(width, LANES)
        return jnp.concatenate([part(first_row), part(first_row + rr)], axis=-1)

    def start_copy(ids_ref, tok, k, slot, j):
        e = ids_ref[tok, k]
        pltpu.make_async_copy(tab_ref.at[e], bufs[slot].at[j * n_sel + k], sem_ref.at[slot]).start()

    def wait_group(slot):
        pltpu.make_async_copy(tab_ref.at[pl.ds(0, grp * n_sel)], bufs[slot], sem_ref.at[slot]).wait()

    def start_group_rolled(ids_ref, group, slot):
        for j in range(grp):
            def issue_chunk(c, carry, j=j):
                for kk in range(PEER_ISSUE_UNROLL):
                    start_copy(ids_ref, group * grp + j, c * PEER_ISSUE_UNROLL + kk, slot, j)
                return carry
            lax.fori_loop(0, n_sel // PEER_ISSUE_UNROLL, issue_chunk, 0)

    lane_r = lax.broadcasted_iota(jnp.int32, (rr, width), 1) % rr
    diag = lane_r == lax.broadcasted_iota(jnp.int32, (rr, width), 0)
    col_k = lax.broadcasted_iota(jnp.int32, (n_sel, width), 1) // rr
    expand = (col_k == lax.broadcasted_iota(jnp.int32, (n_sel, width), 0)).astype(_bf16)
    nt = (((1,), (1,)), ((), ()))

    def score_partial(g, slot, j, prefetch):
        base = g * grp
        if prefetch:
            for k in range(n_sel):
                start_copy(idx_ref, base + PEER_AHEAD * grp + j, k, (slot + PEER_AHEAD) % PEER_RING, j)
        u = slab_matrix(slot, j, 0)
        prod = lax.dot_general(widen(x_ref[base + j]), u, nt, preferred_element_type=_f32)
        return jnp.sum(jnp.where(diag, prod, 0.0), axis=0, keepdims=True)

    def group_scores(partial):
        p_hi = partial.astype(_bf16)
        p_lo = (partial - p_hi.astype(_f32)).astype(_bf16)
        both = lax.dot_general(jnp.concatenate([p_hi, p_lo], axis=0), expand, nt, preferred_element_type=_f32)
        return both[:grp] + both[grp:]

    def group_activations(scores, g):
        gate = jnp.concatenate([gate_ref[g * grp + j] for j in range(grp)], axis=0)
        act = jax.nn.gelu(scores) * gate
        return jnp.dot(act.astype(_bf16), expand, preferred_element_type=_f32)

    def group_outputs(act_wide, g, slot):
        ys = []
        for j in range(grp):
            v = slab_matrix(slot, j, 2 * rr)
            lhs = jnp.where(diag, act_wide[j:j + 1, :], 0.0).astype(_bf16)
            out = jnp.dot(lhs, v, preferred_element_type=_f32)
            y = h_ref[g * grp + j] + jnp.concatenate([out[:, :LANES], out[:, LANES:]], axis=0)
            if emit_h:
                o_ref[g * grp + j] = y
            ys.append(y)
        return tuple(ys)

    def finish_group(partial, g, slot):
        return group_outputs(group_activations(group_scores(partial), g), g, slot)

    def norm_group(ys, g):
        for j, y in enumerate(ys):
            ms = jnp.sum(jnp.sum(y * y, axis=1, keepdims=True), axis=0, keepdims=True) * (1.0 / (rows * LANES))
            n_ref[g * grp + j] = (y * lax.rsqrt(ms + RMS_EPS) * g_ref[...]).astype(n_ref.dtype)

    def group_step(g, slot, partial_prev, ys_prev2, next_block_group=None):
        wait_group(slot)
        if next_block_group is not None:
            @pl.when(step < last_step)
            def _():
                start_group_rolled(idx_next_ref, next_block_group, (slot + PEER_AHEAD) % PEER_RING)
        prefetch = next_block_group is None
        prev, prev_slot = g - 1, (slot - 1) % PEER_RING
        stages = [] if partial_prev is None else [group_scores, lambda s: group_activations(s, prev)]
        state = partial_prev
        partial = []
        for j in range(grp):
            if j < len(stages):
                state = stages[j](state)
            partial.append(score_partial(g, slot, j, prefetch))
        for stage in stages[grp:]:
            state = stage(state)
        ys = None if partial_prev is None else group_outputs(state, prev, prev_slot)
        if ys_prev2 is not None:
            norm_group(ys_prev2, g - 2)
        return jnp.concatenate(partial, axis=0), ys

    @pl.when(step == 0)
    def _():
        for slot in range(PEER_AHEAD):
            start_group_rolled(idx_ref, slot, slot)

    partial, _ = group_step(0, 0, None, None)
    carry = group_step(1, 1, partial, None)

    def ring_body(it, carry):
        for s in range(2, PEER_RING + 2):
            carry = group_step(it * PEER_RING + s, s % PEER_RING, *carry)
        return carry

    carry = lax.fori_loop(0, n_groups // PEER_RING - 1, ring_body, carry)
    for s in range(2, PEER_RING):
        g = n_groups - PEER_RING + s
        carry = group_step(g, s, *carry, next_block_group=g + PEER_AHEAD - n_groups)
    partial, ys = carry
    norm_group(ys, n_groups - 2)
    norm_group(finish_group(partial, n_groups - 1, PEER_RING - 1), n_groups - 1)


def _peer_experts(idx, gates, xn, h, table, norm_g, norm_dtype, emit_h):
    t, d = xn.shape
    rows = d // LANES
    n_sel = idx.shape[1]
    tb = _blk(t, PEER_TOKENS_PER_STEP)
    n_steps = t // tb
    n_groups = tb // PEER_TOKENS_PER_GROUP
    assert n_groups % PEER_RING == 0 and n_groups >= 2 * PEER_RING
    assert n_sel % PEER_ISSUE_UNROLL == 0 and rows % (2 * BF16_TILE_ROWS) == 0
    slot_shape = (PEER_TOKENS_PER_GROUP * n_sel, 2 * rows, LANES)
    row_spec = pl.BlockSpec((tb, rows, LANES), lambda i: (i, 0, 0))
    out_specs = [row_spec]
    out_shape = [jax.ShapeDtypeStruct((t, rows, LANES), norm_dtype)]
    if emit_h:
        out_specs.insert(0, row_spec)
        out_shape.insert(0, jax.ShapeDtypeStruct((t, rows, LANES), _f32))
    outs = pl.pallas_call(
        functools.partial(_peer_kernel, n_sel=n_sel, emit_h=emit_h),
        grid=(n_steps,),
        in_specs=[
            pl.BlockSpec((tb, n_sel), lambda i: (i, 0), memory_space=pltpu.SMEM),
            pl.BlockSpec((tb, n_sel), lambda i: (jnp.minimum(i + 1, n_steps - 1), 0), memory_space=pltpu.SMEM),
            pl.BlockSpec((tb, 1, n_sel), lambda i: (i, 0, 0)),
            row_spec,
            row_spec,
            pl.BlockSpec((rows, LANES), lambda i: (0, 0)),
            pl.BlockSpec(memory_space=pl.ANY),
        ],
        out_specs=out_specs,
        out_shape=out_shape,
        scratch_shapes=[pltpu.VMEM(slot_shape, _bf16) for _ in range(PEER_RING)]
                       + [pltpu.SemaphoreType.DMA((PEER_RING,))],
        compiler_params=_params("arbitrary"),
        name="peer_experts",
    )(idx, idx, gates.reshape(t, 1, n_sel), xn.reshape(t, rows, LANES), h.reshape(t, rows, LANES),
      norm_g.reshape(rows, LANES), table)
    h_new = outs[0].reshape(t, d) if emit_h else None
    return h_new, outs[-1].reshape(t, d)


def _expert_table(expert_u, expert_v):
    e, d = expert_u.shape
    rows = d // LANES
    return jnp.concatenate([expert_u.astype(_bf16).reshape(e, rows, LANES),
                            expert_v.astype(_bf16).reshape(e, rows, LANES)], axis=1)


def kernel(x, mix_norm, w_in, conv_w, conv_b, w_rg, b_rg, w_ig, b_ig, lru_lambda, w_pool, pool_scale,
           w_out, ffn_norm, w_query, sub_keys, expert_u, expert_v, final_norm):
    batch, seq, d = x.shape
    depth = w_in.shape[0]
    t = batch * seq
    h = x.reshape(t, d)
    xn = _rmsnorm(h, mix_norm[0], _bf16)
    for l in range(depth):
        last = l == depth - 1
        z = _matmul(xn, w_in, l, name="in_proj")
        ypool = _pool_branch(z, w_pool[l].astype(_bf16), pool_scale[l], batch, seq, d)
        merged = _lru_branch(z, ypool, conv_w[l], conv_b[l], w_rg[l].astype(_bf16), b_rg[l],
                             w_ig[l].astype(_bf16), b_ig[l], lru_lambda[l], batch, seq, d)
        h = _matmul(merged, w_out, l, residual=h, name="out_proj")
        xn = _rmsnorm(h, ffn_norm[l], _bf16)
        idx, gates = _route(xn, w_query, l, sub_keys[l].astype(_bf16))
        h, xn = _peer_experts(idx, gates, xn, h, _expert_table(expert_u[l], expert_v[l]),
                              final_norm if last else mix_norm[l + 1], _f32 if last else _bf16, emit_h=not last)
    return xn.reshape(batch, seq, d)
```

```python
import functools

import jax
import jax.numpy as jnp
from jax import lax
from jax.experimental import pallas as pl
from jax.experimental.pallas import tpu as pltpu

RMS_EPS = 1e-6
RG_C = 8.0
CONV_WIDTH = 4
POOL_WINDOWS = (2, 4, 8, 16)
PEER_TOPK = 16

LANES = 128
SUBLANES = 8
VMEM_LIMIT_BYTES = 56 * 1024 * 1024
HALO_ROWS_POOL = 16
HALO_ROWS_CONV = 8
POOL_TIME_BLOCK = 256
LRU_TIME_BLOCK = 512

_f32 = jnp.float32
_bf16 = jnp.bfloat16


def _params(*sem):
    return pltpu.CompilerParams(dimension_semantics=sem, vmem_limit_bytes=VMEM_LIMIT_BYTES)


def _blk(dim, pref):
    return pref if dim % pref == 0 else dim


def _rmsnorm_kernel(x_ref, g_ref, o_ref):
    x = x_ref[...]
    ms = jnp.mean(x * x, axis=-1, keepdims=True)
    o_ref[...] = (x * lax.rsqrt(ms + RMS_EPS) * g_ref[...]).astype(o_ref.dtype)


def _rmsnorm(x, g, out_dtype):
    t, d = x.shape
    tm = _blk(t, 256)
    return pl.pallas_call(
        _rmsnorm_kernel,
        grid=(t // tm,),
        in_specs=[pl.BlockSpec((tm, d), lambda i: (i, 0)),
                  pl.BlockSpec((1, d), lambda i: (0, 0))],
        out_specs=pl.BlockSpec((tm, d), lambda i: (i, 0)),
        out_shape=jax.ShapeDtypeStruct((t, d), out_dtype),
        compiler_params=_params("parallel"),
        name="rmsnorm",
    )(x, g.reshape(1, d))


def _matmul_kernel(a_ref, w_ref, o_ref):
    w = w_ref[...].astype(_bf16)
    o_ref[...] = jnp.dot(a_ref[...], w, preferred_element_type=_f32).astype(o_ref.dtype)


def _matmul_res_kernel(a_ref, w_ref, r_ref, o_ref):
    w = w_ref[...].astype(_bf16)
    o_ref[...] = r_ref[...] + jnp.dot(a_ref[...], w, preferred_element_type=_f32)


def _matmul(a, w_stack, layer, residual=None, out_dtype=_f32, name="matmul"):
    m, k = a.shape
    n = w_stack.shape[2]
    tm = _blk(m, 1024)
    tn = _blk(n, 512)
    in_specs = [pl.BlockSpec((tm, k), lambda i, j: (i, 0)),
                pl.BlockSpec((None, k, tn), lambda i, j: (layer, 0, j))]
    w = w_stack
    args = [a, w]
    kern = _matmul_kernel
    if residual is not None:
        in_specs.append(pl.BlockSpec((tm, tn), lambda i, j: (i, j)))
        args.append(residual)
        kern = _matmul_res_kernel
    return pl.pallas_call(
        kern,
        grid=(m // tm, n // tn),
        in_specs=in_specs,
        out_specs=pl.BlockSpec((tm, tn), lambda i, j: (i, j)),
        out_shape=jax.ShapeDtypeStruct((m, n), out_dtype),
        compiler_params=_params("parallel", "arbitrary"),
        name=name,
    )(*args)


def _pool_kernel(x_ref, halo_ref, gate_ref, w_ref, scale_ref, o_ref, *, group_dim):
    t = pl.program_id(1)
    ts = x_ref.shape[0]
    row = lax.broadcasted_iota(jnp.int32, (ts, 1), 0) + t * ts
    first = t == 0
    for g, win in enumerate(POOL_WINDOWS):
        cols = slice(g * group_dim, (g + 1) * group_dim)
        x = x_ref[:, cols]
        halo = jnp.where(first, 0.0, halo_ref[:, cols])
        acc = jnp.concatenate([halo, x], axis=0)
        step = 1
        while step < win:
            acc = acc + pltpu.roll(acc, step, axis=0)
            step *= 2
        wsum = acc[HALO_ROWS_POOL:]
        count = jnp.minimum(row + 1, win).astype(_f32)
        pooled = wsum / count - x
        y = jnp.dot(pooled.astype(_bf16), w_ref[g], preferred_element_type=_f32)
        y = y * scale_ref[:, cols]
        o_ref[:, cols] = (jax.nn.sigmoid(gate_ref[:, cols]) * y).astype(o_ref.dtype)


def _pool_branch(z, w_pool, pool_scale, batch, seq, d):
    t = batch * seq
    groups, group_dim = w_pool.shape[0], w_pool.shape[1]
    ts = _blk(seq, POOL_TIME_BLOCK)
    nt = seq // ts
    hb = ts // HALO_ROWS_POOL
    x_col, gate_col = 2, 4
    return pl.pallas_call(
        functools.partial(_pool_kernel, group_dim=group_dim),
        grid=(batch, nt),
        in_specs=[
            pl.BlockSpec((ts, d), lambda b, s: (b * nt + s, x_col)),
            pl.BlockSpec((HALO_ROWS_POOL, d),
                         lambda b, s: (jnp.maximum((b * nt + s) * hb - 1, 0), x_col)),
            pl.BlockSpec((ts, d), lambda b, s: (b * nt + s, gate_col)),
            pl.BlockSpec((groups, group_dim, group_dim), lambda b, s: (0, 0, 0)),
            pl.BlockSpec((1, d), lambda b, s: (0, 0)),
        ],
        out_specs=pl.BlockSpec((ts, d), lambda b, s: (b * nt + s, 0)),
        out_shape=jax.ShapeDtypeStruct((t, d), _bf16),
        compiler_params=_params("parallel", "arbitrary"),
        name="pool_branch",
    )(z, z, z, w_pool, pool_scale.reshape(1, d))


def _lru_kernel(x_ref, halo_ref, gelu_ref, glru_ref, ypool_ref, convw_ref, convb_ref,
                wrg_ref, brg_ref, wig_ref, big_ref, lam_ref, o_ref, carry_ref):
    t = pl.program_id(2)
    ts = x_ref.shape[0]

    @pl.when(t == 0)
    def _():
        carry_ref[...] = jnp.zeros_like(carry_ref)

    x = x_ref[...]
    halo = jnp.where(t == 0, 0.0, halo_ref[...])
    ext = jnp.concatenate([halo, x], axis=0)
    conv = convb_ref[...] + x * convw_ref[CONV_WIDTH - 1:CONV_WIDTH, :]
    for k in range(CONV_WIDTH - 1):
        shifted = pltpu.roll(ext, CONV_WIDTH - 1 - k, axis=0)[HALO_ROWS_CONV:]
        conv = conv + shifted * convw_ref[k:k + 1, :]

    cb = conv.astype(_bf16)
    r = jax.nn.sigmoid(jnp.dot(cb, wrg_ref[0], preferred_element_type=_f32) + brg_ref[...])
    i = jax.nn.sigmoid(jnp.dot(cb, wig_ref[0], preferred_element_type=_f32) + big_ref[...])
    neg_lam = -lam_ref[...]
    softplus = jnp.maximum(neg_lam, 0.0) + jnp.log(1.0 + jnp.exp(-jnp.abs(neg_lam)))
    log_a = -RG_C * r * softplus
    a = jnp.exp(log_a)
    u = jnp.sqrt(1.0 - a * a) * (i * conv)

    tiles = ts // SUBLANES
    a = a.reshape(tiles, SUBLANES, a.shape[-1])
    u = u.reshape(tiles, SUBLANES, u.shape[-1])
    sub = lax.broadcasted_iota(jnp.int32, a.shape, 1)
    step = 1
    while step < SUBLANES:
        keep = sub >= step
        a_prev = jnp.where(keep, pltpu.roll(a, step, axis=1), 1.0)
        u_prev = jnp.where(keep, pltpu.roll(u, step, axis=1), 0.0)
        u = a * u_prev + u
        a = a * a_prev
        step *= 2
    carry = carry_ref[...]
    hs = []
    for i in range(tiles):
        hs.append(a[i] * carry + u[i])
        carry = hs[-1][SUBLANES - 1:SUBLANES, :]
    carry_ref[...] = carry
    h = jnp.concatenate(hs, axis=0)

    y = h * jax.nn.gelu(gelu_ref[...])
    merged = jax.nn.sigmoid(glru_ref[...]) * y + ypool_ref[...].astype(_f32)
    o_ref[...] = merged.astype(o_ref.dtype)


def _lru_branch(z, ypool, conv_w, conv_b, w_rg, b_rg, w_ig, b_ig, lam, batch, seq, d):
    t = batch * seq
    heads, hd = w_rg.shape[0], w_rg.shape[1]
    ts = _blk(seq, LRU_TIME_BLOCK)
    nt = seq // ts
    hb = ts // HALO_ROWS_CONV
    nh = d // hd
    row = lambda b, h, s: b * nt + s
    vec = pl.BlockSpec((1, hd), lambda b, h, s: (0, h))
    return pl.pallas_call(
        _lru_kernel,
        grid=(batch, heads, nt),
        in_specs=[
            pl.BlockSpec((ts, hd), lambda b, h, s: (row(b, h, s), h)),
            pl.BlockSpec((HALO_ROWS_CONV, hd),
                         lambda b, h, s: (jnp.maximum(row(b, h, s) * hb - 1, 0), h)),
            pl.BlockSpec((ts, hd), lambda b, h, s: (row(b, h, s), nh + h)),
            pl.BlockSpec((ts, hd), lambda b, h, s: (row(b, h, s), 3 * nh + h)),
            pl.BlockSpec((ts, hd), lambda b, h, s: (row(b, h, s), h)),
            pl.BlockSpec((CONV_WIDTH, hd), lambda b, h, s: (0, h)),
            vec,
            pl.BlockSpec((1, hd, hd), lambda b, h, s: (h, 0, 0)),
            vec,
            pl.BlockSpec((1, hd, hd), lambda b, h, s: (h, 0, 0)),
            vec,
            vec,
        ],
        out_specs=pl.BlockSpec((ts, hd), lambda b, h, s: (row(b, h, s), h)),
        out_shape=jax.ShapeDtypeStruct((t, d), _bf16),
        scratch_shapes=[pltpu.VMEM((1, hd), _f32)],
        compiler_params=_params("parallel", "parallel", "arbitrary"),
        name="lru_branch",
    )(z, z, z, z, ypool, conv_w, conv_b.reshape(1, d), w_rg, b_rg.reshape(1, d),
      w_ig, b_ig.reshape(1, d), lam.reshape(1, d))


def _topk_rows(s, k, ids=None):
    if ids is None:
        ids = lax.broadcasted_iota(jnp.int32, s.shape, 0).astype(_f32)
    big = float(2 ** 20)
    vals, idxs = [], []
    for _ in range(k):
        m = jnp.max(s, axis=0, keepdims=True)
        idx = jnp.min(jnp.where(s == m, ids, big), axis=0, keepdims=True)
        vals.append(m)
        idxs.append(idx)
        s = jnp.where(ids == idx, -jnp.inf, s)
    return jnp.concatenate(vals, axis=0), jnp.concatenate(idxs, axis=0)


def _product_candidates(v1, v2):
    k = PEER_TOPK
    half = k // 2
    n = v1.shape[1]
    sub = lax.broadcasted_iota(jnp.int32, (half, n), 0)
    blocks = [v1[0:1] + v2]
    ids = [lax.broadcasted_iota(jnp.int32, (k, n), 0)]
    blocks.append(v1[1:2] + v2[0:half])
    ids.append(k + sub)
    blocks.append(v1[half:k] + v2[0:1])
    ids.append((sub + half) * k)
    for b in range(k // 3):
        ok = (sub >= 2) & ((sub + 1) * (b + 1) <= k)
        blocks.append(jnp.where(ok, v1[0:half] + v2[b:b + 1], -jnp.inf))
        ids.append(sub * k + b)
    return jnp.concatenate(blocks, axis=0), jnp.concatenate(ids, axis=0).astype(_f32)


def _route_kernel(x_ref, wq_ref, keys_ref, idx_ref, gate_ref, wq_bf16_ref, *, n_keys):
    half = keys_ref.shape[-1]
    @pl.when(pl.program_id(1) == 0)
    def _():
        wq_bf16_ref[...] = wq_ref[...].astype(_bf16)

    q = jnp.dot(x_ref[...], wq_bf16_ref[...], preferred_element_type=_f32).astype(_bf16)
    nt = (((1,), (1,)), ((), ()))
    s1 = lax.dot_general(keys_ref[0, 0], q[:, :half], nt, preferred_element_type=_f32)
    s2 = lax.dot_general(keys_ref[0, 1], q[:, half:], nt, preferred_element_type=_f32)
    v1, i1 = _topk_rows(s1, PEER_TOPK)
    v2, i2 = _topk_rows(s2, PEER_TOPK)
    cand, cand_ids = _product_candidates(v1, v2)
    vals, pos = _topk_rows(cand, PEER_TOPK, cand_ids)
    pos_hi = jnp.floor(pos * (1.0 / PEER_TOPK))
    pos_lo = pos - pos_hi * PEER_TOPK
    e1 = jnp.zeros_like(pos)
    e2 = jnp.zeros_like(pos)
    for a in range(PEER_TOPK):
        e1 = e1 + jnp.where(pos_hi == a, i1[a:a + 1], 0.0)
        e2 = e2 + jnp.where(pos_lo == a, i2[a:a + 1], 0.0)
    idx_ref[0] = (e1 * n_keys + e2).astype(jnp.int32)
    ex = jnp.exp(vals - vals[0:1])
    gate_ref[0] = ex / jnp.sum(ex, axis=0, keepdims=True)


def _route(xn, w_query_stack, layer, sub_keys):
    t, d = xn.shape
    heads, _, n_keys, half = sub_keys.shape
    qd = 2 * half
    tm = _blk(t, 256)
    out_sds = lambda dt: jax.ShapeDtypeStruct((heads, PEER_TOPK, t), dt)
    idx, gates = pl.pallas_call(
        functools.partial(_route_kernel, n_keys=n_keys),
        grid=(heads, t // tm),
        in_specs=[pl.BlockSpec((tm, d), lambda h, i: (i, 0)),
                  pl.BlockSpec((None, d, qd), lambda h, i: (layer, 0, h)),
                  pl.BlockSpec((1, 2, n_keys, half), lambda h, i: (h, 0, 0, 0))],
        out_specs=[pl.BlockSpec((1, PEER_TOPK, tm), lambda h, i: (h, 0, i)),
                   pl.BlockSpec((1, PEER_TOPK, tm), lambda h, i: (h, 0, i))],
        out_shape=[out_sds(jnp.int32), out_sds(_f32)],
        scratch_shapes=[pltpu.VMEM((d, qd), _bf16)],
        compiler_params=_params("parallel", "arbitrary"),
        name="peer_route",
    )(xn, w_query_stack, sub_keys)
    flat = lambda a: jnp.transpose(a, (2, 0, 1)).reshape(t, heads * PEER_TOPK)
    return flat(idx), flat(gates)


PEER_TOKENS_PER_STEP = 128
PEER_TOKENS_PER_GROUP = 2
PEER_RING = 4
PEER_AHEAD = 2
PEER_ISSUE_UNROLL = 8
BF16_TILE_ROWS = 16
DMA_PRIORITIES = 2


def _peer_kernel(idx_ref, idx_next_ref, gate_ref, x_ref, h_ref, g_ref, tab_ref, *rest, n_sel, emit_h):
    n_out = 2 if emit_h else 1
    outs, scratch = rest[:n_out], rest[n_out:]
    o_ref = outs[0] if emit_h else None
    n_ref = outs[-1]
    bufs, sem_ref = scratch[:PEER_RING], scratch[PEER_RING]
    tb, rows, _ = x_ref.shape
    rr = rows // 2
    grp = PEER_TOKENS_PER_GROUP
    n_groups = tb // grp
    width = n_sel * rr
    step = pl.program_id(0)
    last_step = pl.num_programs(0) - 1

    def widen(a):
        return jnp.concatenate([a[..., 0:rr, :], a[..., rr:2 * rr, :]], axis=-1)

    def slab_matrix(slot, j, first_row):
        part = lambda r0: bufs[slot][pl.ds(j * n_sel, n_sel), r0:r0 + rr, :].reshape(width, LANES)
        return jnp.concatenate([part(first_row), part(first_row + rr)], axis=-1)

    def start_copy(ids_ref, tok, k, slot, j, priority):
        e = ids_ref[tok, k]
        pltpu.make_async_copy(tab_ref.at[e], bufs[slot].at[j * n_sel + k], sem_ref.at[slot]).start(priority=priority)

    def wait_group(slot):
        pltpu.make_async_copy(tab_ref.at[pl.ds(0, grp * n_sel)], bufs[slot], sem_ref.at[slot]).wait()

    def start_group_rolled(ids_ref, group, slot):
        for j in range(grp):
            def issue_chunk(c, carry, j=j):
                for kk in range(PEER_ISSUE_UNROLL):
                    start_copy(ids_ref, group * grp + j, c * PEER_ISSUE_UNROLL + kk, slot, j, kk % DMA_PRIORITIES)
                return carry
            lax.fori_loop(0, n_sel // PEER_ISSUE_UNROLL, issue_chunk, 0)

    lane_r = lax.broadcasted_iota(jnp.int32, (rr, width), 1) % rr
    diag = lane_r == lax.broadcasted_iota(jnp.int32, (rr, width), 0)
    col_k = lax.broadcasted_iota(jnp.int32, (n_sel, width), 1) // rr
    expand = (col_k == lax.broadcasted_iota(jnp.int32, (n_sel, width), 0)).astype(_bf16)
    nt = (((1,), (1,)), ((), ()))

    def score_partial(g, slot, j, prefetch):
        base = g * grp
        if prefetch:
            for k in range(n_sel):
                start_copy(idx_ref, base + PEER_AHEAD * grp + j, k, (slot + PEER_AHEAD) % PEER_RING, j,
                           k % DMA_PRIORITIES)
        u = slab_matrix(slot, j, 0)
        prod = lax.dot_general(widen(x_ref[base + j]), u, nt, preferred_element_type=_f32)
        return jnp.sum(jnp.where(diag, prod, 0.0), axis=0, keepdims=True)

    def group_scores(partial):
        p_hi = partial.astype(_bf16)
        p_lo = (partial - p_hi.astype(_f32)).astype(_bf16)
        both = lax.dot_general(jnp.concatenate([p_hi, p_lo], axis=0), expand, nt, preferred_element_type=_f32)
        return both[:grp] + both[grp:]

    def group_activations(scores, g):
        gate = jnp.concatenate([gate_ref[g * grp + j] for j in range(grp)], axis=0)
        act = jax.nn.gelu(scores) * gate
        return jnp.dot(act.astype(_bf16), expand, preferred_element_type=_f32)

    def group_outputs(act_wide, g, slot):
        ys = []
        for j in range(grp):
            v = slab_matrix(slot, j, 2 * rr)
            lhs = jnp.where(diag, act_wide[j:j + 1, :], 0.0).astype(_bf16)
            out = jnp.dot(lhs, v, preferred_element_type=_f32)
            y = h_ref[g * grp + j] + jnp.concatenate([out[:, :LANES], out[:, LANES:]], axis=0)
            if emit_h:
                o_ref[g * grp + j] = y
            ys.append(y)
        return tuple(ys)

    def finish_group(partial, g, slot):
        return group_outputs(group_activations(group_scores(partial), g), g, slot)

    def norm_group(ys, g):
        for j, y in enumerate(ys):
            ms = jnp.sum(jnp.sum(y * y, axis=1, keepdims=True), axis=0, keepdims=True) * (1.0 / (rows * LANES))
            n_ref[g * grp + j] = (y * lax.rsqrt(ms + RMS_EPS) * g_ref[...]).astype(n_ref.dtype)

    def group_step(g, slot, partial_prev, ys_prev2, next_block_group=None):
        wait_group(slot)
        if next_block_group is not None:
            @pl.when(step < last_step)
            def _():
                start_group_rolled(idx_next_ref, next_block_group, (slot + PEER_AHEAD) % PEER_RING)
        prefetch = next_block_group is None
        prev, prev_slot = g - 1, (slot - 1) % PEER_RING
        stages = [] if partial_prev is None else [group_scores, lambda s: group_activations(s, prev)]
        state = partial_prev
        partial = []
        for j in range(grp):
            if j < len(stages):
                state = stages[j](state)
            partial.append(score_partial(g, slot, j, prefetch))
        for stage in stages[grp:]:
            state = stage(state)
        ys = None if partial_prev is None else group_outputs(state, prev, prev_slot)
        if ys_prev2 is not None:
            norm_group(ys_prev2, g - 2)
        return jnp.concatenate(partial, axis=0), ys

    @pl.when(step == 0)
    def _():
        for slot in range(PEER_AHEAD):
            start_group_rolled(idx_ref, slot, slot)

    partial, _ = group_step(0, 0, None, None)
    carry = group_step(1, 1, partial, None)

    def ring_body(it, carry):
        for s in range(2, PEER_RING + 2):
            carry = group_step(it * PEER_RING + s, s % PEER_RING, *carry)
        return carry

    carry = lax.fori_loop(0, n_groups // PEER_RING - 1, ring_body, carry)
    for s in range(2, PEER_RING):
        g = n_groups - PEER_RING + s
        carry = group_step(g, s, *carry, next_block_group=g + PEER_AHEAD - n_groups)
    partial, ys = carry
    norm_group(ys, n_groups - 2)
    norm_group(finish_group(partial, n_groups - 1, PEER_RING - 1), n_groups - 1)


def _peer_experts(idx, gates, xn, h, table, norm_g, norm_dtype, emit_h):
    t, d = xn.shape
    rows = d // LANES
    n_sel = idx.shape[1]
    tb = _blk(t, PEER_TOKENS_PER_STEP)
    n_steps = t // tb
    n_groups = tb // PEER_TOKENS_PER_GROUP
    assert n_groups % PEER_RING == 0 and n_groups >= 2 * PEER_RING
    assert n_sel % PEER_ISSUE_UNROLL == 0 and rows % (2 * BF16_TILE_ROWS) == 0
    slot_shape = (PEER_TOKENS_PER_GROUP * n_sel, 2 * rows, LANES)
    row_spec = pl.BlockSpec((tb, rows, LANES), lambda i: (i, 0, 0))
    out_specs = [row_spec]
    out_shape = [jax.ShapeDtypeStruct((t, rows, LANES), norm_dtype)]
    if emit_h:
        out_specs.insert(0, row_spec)
        out_shape.insert(0, jax.ShapeDtypeStruct((t, rows, LANES), _f32))
    outs = pl.pallas_call(
        functools.partial(_peer_kernel, n_sel=n_sel, emit_h=emit_h),
        grid=(n_steps,),
        in_specs=[
            pl.BlockSpec((tb, n_sel), lambda i: (i, 0), memory_space=pltpu.SMEM),
            pl.BlockSpec((tb, n_sel), lambda i: (jnp.minimum(i + 1, n_steps - 1), 0), memory_space=pltpu.SMEM),
            pl.BlockSpec((tb, 1, n_sel), lambda i: (i, 0, 0)),
            row_spec,
            row_spec,
            pl.BlockSpec((rows, LANES), lambda i: (0, 0)),
            pl.BlockSpec(memory_space=pl.ANY),
        ],
        out_specs=out_specs,
        out_shape=out_shape,
        scratch_shapes=[pltpu.VMEM(slot_shape, _bf16) for _ in range(PEER_RING)]
                       + [pltpu.SemaphoreType.DMA((PEER_RING,))],
        compiler_params=_params("arbitrary"),
        name="peer_experts",
    )(idx, idx, gates.reshape(t, 1, n_sel), xn.reshape(t, rows, LANES), h.reshape(t, rows, LANES),
      norm_g.reshape(rows, LANES), table)
    h_new = outs[0].reshape(t, d) if emit_h else None
    return h_new, outs[-1].reshape(t, d)


def _expert_table(expert_u, expert_v):
    e, d = expert_u.shape
    rows = d // LANES
    return jnp.concatenate([expert_u.astype(_bf16).reshape(e, rows, LANES),
                            expert_v.astype(_bf16).reshape(e, rows, LANES)], axis=1)


def kernel(x, mix_norm, w_in, conv_w, conv_b, w_rg, b_rg, w_ig, b_ig, lru_lambda, w_pool, pool_scale,
           w_out, ffn_norm, w_query, sub_keys, expert_u, expert_v, final_norm):
    batch, seq, d = x.shape
    depth = w_in.shape[0]
    t = batch * seq
    h = x.reshape(t, d)
    xn = _rmsnorm(h, mix_norm[0], _bf16)
    for l in range(depth):
        last = l == depth - 1
        z = _matmul(xn, w_in, l, name="in_proj")
        ypool = _pool_branch(z, w_pool[l].astype(_bf16), pool_scale[l], batch, seq, d)
        merged = _lru_branch(z, ypool, conv_w[l], conv_b[l], w_rg[l].astype(_bf16), b_rg[l],
                             w_ig[l].astype(_bf16), b_ig[l], lru_lambda[l], batch, seq, d)
        h = _matmul(merged, w_out, l, residual=h, name="out_proj")
        xn = _rmsnorm(h, ffn_norm[l], _bf16)
        idx, gates = _route(xn, w_query, l, sub_keys[l].astype(_bf16))
        h, xn = _peer_experts(idx, gates, xn, h, _expert_table(expert_u[l], expert_v[l]),
                              final_norm if last else mix_norm[l + 1], _f32 if last else _bf16, emit_h=not last)
    return xn.reshape(batch, seq, d)
```

```python
import functools

import jax
import jax.numpy as jnp
from jax import lax
from jax.experimental import pallas as pl
from jax.experimental.pallas import tpu as pltpu

RMS_EPS = 1e-6
RG_C = 8.0
CONV_WIDTH = 4
POOL_WINDOWS = (2, 4, 8, 16)
PEER_TOPK = 16

LANES = 128
SUBLANES = 8
VMEM_LIMIT_BYTES = 56 * 1024 * 1024
HALO_ROWS_POOL = 16
HALO_ROWS_CONV = 8
POOL_TIME_BLOCK = 256
LRU_TIME_BLOCK = 512

_f32 = jnp.float32
_bf16 = jnp.bfloat16


def _params(*sem):
    return pltpu.CompilerParams(dimension_semantics=sem, vmem_limit_bytes=VMEM_LIMIT_BYTES)


def _blk(dim, pref):
    return pref if dim % pref == 0 else dim


def _rmsnorm_kernel(x_ref, g_ref, o_ref):
    x = x_ref[...]
    ms = jnp.mean(x * x, axis=-1, keepdims=True)
    o_ref[...] = (x * lax.rsqrt(ms + RMS_EPS) * g_ref[...]).astype(o_ref.dtype)


def _rmsnorm(x, g, out_dtype):
    t, d = x.shape
    tm = _blk(t, 256)
    return pl.pallas_call(
        _rmsnorm_kernel,
        grid=(t // tm,),
        in_specs=[pl.BlockSpec((tm, d), lambda i: (i, 0)),
                  pl.BlockSpec((1, d), lambda i: (0, 0))],
        out_specs=pl.BlockSpec((tm, d), lambda i: (i, 0)),
        out_shape=jax.ShapeDtypeStruct((t, d), out_dtype),
        compiler_params=_params("parallel"),
        name="rmsnorm",
    )(x, g.reshape(1, d))


def _matmul_kernel(a_ref, w_ref, o_ref):
    w = w_ref[...].astype(_bf16)
    o_ref[...] = jnp.dot(a_ref[...], w, preferred_element_type=_f32).astype(o_ref.dtype)


def _matmul_res_kernel(a_ref, w_ref, r_ref, o_ref):
    w = w_ref[...].astype(_bf16)
    o_ref[...] = r_ref[...] + jnp.dot(a_ref[...], w, preferred_element_type=_f32)


def _matmul(a, w_stack, layer, residual=None, out_dtype=_f32, name="matmul"):
    m, k = a.shape
    n = w_stack.shape[2]
    tm = _blk(m, 1024)
    tn = _blk(n, 512)
    in_specs = [pl.BlockSpec((tm, k), lambda i, j: (i, 0)),
                pl.BlockSpec((None, k, tn), lambda i, j: (layer, 0, j))]
    w = w_stack
    args = [a, w]
    kern = _matmul_kernel
    if residual is not None:
        in_specs.append(pl.BlockSpec((tm, tn), lambda i, j: (i, j)))
        args.append(residual)
        kern = _matmul_res_kernel
    return pl.pallas_call(
        kern,
        grid=(m // tm, n // tn),
        in_specs=in_specs,
        out_specs=pl.BlockSpec((tm, tn), lambda i, j: (i, j)),
        out_shape=jax.ShapeDtypeStruct((m, n), out_dtype),
        compiler_params=_params("parallel", "arbitrary"),
        name=name,
    )(*args)


def _pool_kernel(x_ref, halo_ref, gate_ref, w_ref, scale_ref, o_ref, *, group_dim):
    t = pl.program_id(1)
    ts = x_ref.shape[0]
    row = lax.broadcasted_iota(jnp.int32, (ts, 1), 0) + t * ts
    first = t == 0
    for g, win in enumerate(POOL_WINDOWS):
        cols = slice(g * group_dim, (g + 1) * group_dim)
        x = x_ref[:, cols]
        halo = jnp.where(first, 0.0, halo_ref[:, cols])
        acc = jnp.concatenate([halo, x], axis=0)
        step = 1
        while step < win:
            acc = acc + pltpu.roll(acc, step, axis=0)
            step *= 2
        wsum = acc[HALO_ROWS_POOL:]
        count = jnp.minimum(row + 1, win).astype(_f32)
        pooled = wsum / count - x
        y = jnp.dot(pooled.astype(_bf16), w_ref[g], preferred_element_type=_f32)
        y = y * scale_ref[:, cols]
        o_ref[:, cols] = (jax.nn.sigmoid(gate_ref[:, cols]) * y).astype(o_ref.dtype)


def _pool_branch(z, w_pool, pool_scale, batch, seq, d):
    t = batch * seq
    groups, group_dim = w_pool.shape[0], w_pool.shape[1]
    ts = _blk(seq, POOL_TIME_BLOCK)
    nt = seq // ts
    hb = ts // HALO_ROWS_POOL
    x_col, gate_col = 2, 4
    return pl.pallas_call(
        functools.partial(_pool_kernel, group_dim=group_dim),
        grid=(batch, nt),
        in_specs=[
            pl.BlockSpec((ts, d), lambda b, s: (b * nt + s, x_col)),
            pl.BlockSpec((HALO_ROWS_POOL, d),
                         lambda b, s: (jnp.maximum((b * nt + s) * hb - 1, 0), x_col)),
            pl.BlockSpec((ts, d), lambda b, s: (b * nt + s, gate_col)),
            pl.BlockSpec((groups, group_dim, group_dim), lambda b, s: (0, 0, 0)),
            pl.BlockSpec((1, d), lambda b, s: (0, 0)),
        ],
        out_specs=pl.BlockSpec((ts, d), lambda b, s: (b * nt + s, 0)),
        out_shape=jax.ShapeDtypeStruct((t, d), _bf16),
        compiler_params=_params("parallel", "arbitrary"),
        name="pool_branch",
    )(z, z, z, w_pool, pool_scale.reshape(1, d))


def _lru_kernel(x_ref, halo_ref, gelu_ref, glru_ref, ypool_ref, convw_ref, convb_ref,
                wrg_ref, brg_ref, wig_ref, big_ref, lam_ref, o_ref, carry_ref):
    t = pl.program_id(2)
    ts = x_ref.shape[0]

    @pl.when(t == 0)
    def _():
        carry_ref[...] = jnp.zeros_like(carry_ref)

    x = x_ref[...]
    halo = jnp.where(t == 0, 0.0, halo_ref[...])
    ext = jnp.concatenate([halo, x], axis=0)
    conv = convb_ref[...] + x * convw_ref[CONV_WIDTH - 1:CONV_WIDTH, :]
    for k in range(CONV_WIDTH - 1):
        shifted = pltpu.roll(ext, CONV_WIDTH - 1 - k, axis=0)[HALO_ROWS_CONV:]
        conv = conv + shifted * convw_ref[k:k + 1, :]

    cb = conv.astype(_bf16)
    r = jax.nn.sigmoid(jnp.dot(cb, wrg_ref[0], preferred_element_type=_f32) + brg_ref[...])
    i = jax.nn.sigmoid(jnp.dot(cb, wig_ref[0], preferred_element_type=_f32) + big_ref[...])
    neg_lam = -lam_ref[...]
    softplus = jnp.maximum(neg_lam, 0.0) + jnp.log(1.0 + jnp.exp(-jnp.abs(neg_lam)))
    log_a = -RG_C * r * softplus
    a = jnp.exp(log_a)
    u = jnp.sqrt(1.0 - a * a) * (i * conv)

    tiles = ts // SUBLANES
    a = a.reshape(tiles, SUBLANES, a.shape[-1])
    u = u.reshape(tiles, SUBLANES, u.shape[-1])
    sub = lax.broadcasted_iota(jnp.int32, a.shape, 1)
    step = 1
    while step < SUBLANES:
        keep = sub >= step
        a_prev = jnp.where(keep, pltpu.roll(a, step, axis=1), 1.0)
        u_prev = jnp.where(keep, pltpu.roll(u, step, axis=1), 0.0)
        u = a * u_prev + u
        a = a * a_prev
        step *= 2
    carry = carry_ref[...]
    hs = []
    for i in range(tiles):
        hs.append(a[i] * carry + u[i])
        carry = hs[-1][SUBLANES - 1:SUBLANES, :]
    carry_ref[...] = carry
    h = jnp.concatenate(hs, axis=0)

    y = h * jax.nn.gelu(gelu_ref[...])
    merged = jax.nn.sigmoid(glru_ref[...]) * y + ypool_ref[...].astype(_f32)
    o_ref[...] = merged.astype(o_ref.dtype)


def _lru_branch(z, ypool, conv_w, conv_b, w_rg, b_rg, w_ig, b_ig, lam, batch, seq, d):
    t = batch * seq
    heads, hd = w_rg.shape[0], w_rg.shape[1]
    ts = _blk(seq, LRU_TIME_BLOCK)
    nt = seq // ts
    hb = ts // HALO_ROWS_CONV
    nh = d // hd
    row = lambda b, h, s: b * nt + s
    vec = pl.BlockSpec((1, hd), lambda b, h, s: (0, h))
    return pl.pallas_call(
        _lru_kernel,
        grid=(batch, heads, nt),
        in_specs=[
            pl.BlockSpec((ts, hd), lambda b, h, s: (row(b, h, s), h)),
            pl.BlockSpec((HALO_ROWS_CONV, hd),
                         lambda b, h, s: (jnp.maximum(row(b, h, s) * hb - 1, 0), h)),
            pl.BlockSpec((ts, hd), lambda b, h, s: (row(b, h, s), nh + h)),
            pl.BlockSpec((ts, hd), lambda b, h, s: (row(b, h, s), 3 * nh + h)),
            pl.BlockSpec((ts, hd), lambda b, h, s: (row(b, h, s), h)),
            pl.BlockSpec((CONV_WIDTH, hd), lambda b, h, s: (0, h)),
            vec,
            pl.BlockSpec((1, hd, hd), lambda b, h, s: (h, 0, 0)),
            vec,
            pl.BlockSpec((1, hd, hd), lambda b, h, s: (h, 0, 0)),
            vec,
            vec,
        ],
        out_specs=pl.BlockSpec((ts, hd), lambda b, h, s: (row(b, h, s), h)),
        out_shape=jax.ShapeDtypeStruct((t, d), _bf16),
        scratch_shapes=[pltpu.VMEM((1, hd), _f32)],
        compiler_params=_params("parallel", "parallel", "arbitrary"),
        name="lru_branch",
    )(z, z, z, z, ypool, conv_w, conv_b.reshape(1, d), w_rg, b_rg.reshape(1, d),
      w_ig, b_ig.reshape(1, d), lam.reshape(1, d))


def _topk_rows(s, k, ids=None):
    if ids is None:
        ids = lax.broadcasted_iota(jnp.int32, s.shape, 0).astype(_f32)
    big = float(2 ** 20)
    vals, idxs = [], []
    for _ in range(k):
        m = jnp.max(s, axis=0, keepdims=True)
        idx = jnp.min(jnp.where(s == m, ids, big), axis=0, keepdims=True)
        vals.append(m)
        idxs.append(idx)
        s = jnp.where(ids == idx, -jnp.inf, s)
    return jnp.concatenate(vals, axis=0), jnp.concatenate(idxs, axis=0)


def _product_candidates(v1, v2):
    k = PEER_TOPK
    half = k // 2
    n = v1.shape[1]
    sub = lax.broadcasted_iota(jnp.int32, (half, n), 0)
    blocks = [v1[0:1] + v2]
    ids = [lax.broadcasted_iota(jnp.int32, (k, n), 0)]
    blocks.append(v1[1:2] + v2[0:half])
    ids.append(k + sub)
    blocks.append(v1[half:k] + v2[0:1])
    ids.append((sub + half) * k)
    for b in range(k // 3):
        ok = (sub >= 2) & ((sub + 1) * (b + 1) <= k)
        blocks.append(jnp.where(ok, v1[0:half] + v2[b:b + 1], -jnp.inf))
        ids.append(sub * k + b)
    return jnp.concatenate(blocks, axis=0), jnp.concatenate(ids, axis=0).astype(_f32)


def _route_kernel(x_ref, wq_ref, keys_ref, idx_ref, gate_ref, wq_bf16_ref, *, n_keys):
    half = keys_ref.shape[-1]
    @pl.when(pl.program_id(1) == 0)
    def _():
        wq_bf16_ref[...] = wq_ref[...].astype(_bf16)

    q = jnp.dot(x_ref[...], wq_bf16_ref[...], preferred_element_type=_f32).astype(_bf16)
    nt = (((1,), (1,)), ((), ()))
    s1 = lax.dot_general(keys_ref[0, 0], q[:, :half], nt, preferred_element_type=_f32)
    s2 = lax.dot_general(keys_ref[0, 1], q[:, half:], nt, preferred_element_type=_f32)
    v1, i1 = _topk_rows(s1, PEER_TOPK)
    v2, i2 = _topk_rows(s2, PEER_TOPK)
    cand, cand_ids = _product_candidates(v1, v2)
    vals, pos = _topk_rows(cand, PEER_TOPK, cand_ids)
    pos_hi = jnp.floor(pos * (1.0 / PEER_TOPK))
    pos_lo = pos - pos_hi * PEER_TOPK
    e1 = jnp.zeros_like(pos)
    e2 = jnp.zeros_like(pos)
    for a in range(PEER_TOPK):
        e1 = e1 + jnp.where(pos_hi == a, i1[a:a + 1], 0.0)
        e2 = e2 + jnp.where(pos_lo == a, i2[a:a + 1], 0.0)
    idx_ref[0] = (e1 * n_keys + e2).astype(jnp.int32)
    ex = jnp.exp(vals - vals[0:1])
    gate_ref[0] = ex / jnp.sum(ex, axis=0, keepdims=True)


def _route(xn, w_query_stack, layer, sub_keys):
    t, d = xn.shape
    heads, _, n_keys, half = sub_keys.shape
    qd = 2 * half
    tm = _blk(t, 256)
    out_sds = lambda dt: jax.ShapeDtypeStruct((heads, PEER_TOPK, t), dt)
    idx, gates = pl.pallas_call(
        functools.partial(_route_kernel, n_keys=n_keys),
        grid=(heads, t // tm),
        in_specs=[pl.BlockSpec((tm, d), lambda h, i: (i, 0)),
                  pl.BlockSpec((None, d, qd), lambda h, i: (layer, 0, h)),
                  pl.BlockSpec((1, 2, n_keys, half), lambda h, i: (h, 0, 0, 0))],
        out_specs=[pl.BlockSpec((1, PEER_TOPK, tm), lambda h, i: (h, 0, i)),
                   pl.BlockSpec((1, PEER_TOPK, tm), lambda h, i: (h, 0, i))],
        out_shape=[out_sds(jnp.int32), out_sds(_f32)],
        scratch_shapes=[pltpu.VMEM((d, qd), _bf16)],
        compiler_params=_params("parallel", "arbitrary"),
        name="peer_route",
    )(xn, w_query_stack, sub_keys)
    flat = lambda a: jnp.transpose(a, (2, 0, 1)).reshape(t, heads * PEER_TOPK)
    return flat(idx), flat(gates)


PEER_TOKENS_PER_STEP = 128
PEER_TOKENS_PER_GROUP = 2
PEER_RING = 4
PEER_AHEAD = 2
PEER_ISSUE_UNROLL = 8
BF16_TILE_ROWS = 16
DMA_PRIORITIES = 2


def _peer_kernel(idx_ref, idx_next_ref, gate_ref, x_ref, h_ref, g_ref, tab_ref, *rest, n_sel, emit_h):
    n_out = 2 if emit_h else 1
    outs, scratch = rest[:n_out], rest[n_out:]
    o_ref = outs[0] if emit_h else None
    n_ref = outs[-1]
    bufs, sem_ref = scratch[:PEER_RING], scratch[PEER_RING]
    tb, rows, _ = x_ref.shape
    rr = rows // 2
    grp = PEER_TOKENS_PER_GROUP
    n_groups = tb // grp
    width = n_sel * rr
    step = pl.program_id(0)
    last_step = pl.num_programs(0) - 1

    def widen(a):
        return jnp.concatenate([a[..., 0:rr, :], a[..., rr:2 * rr, :]], axis=-1)

    def slab_matrix(slot, j, first_row):
        part = lambda r0: bufs[slot][pl.ds(j * n_sel, n_sel), r0:r0 + rr, :].reshape(width, LANES)
        return jnp.concatenate([part(first_row), part(first_row + rr)], axis=-1)

    def start_copy(ids_ref, tok, k, slot, j, priority):
        e = ids_ref[tok, k]
        pltpu.make_async_copy(tab_ref.at[e], bufs[slot].at[j * n_sel + k], sem_ref.at[slot]).start(priority=priority)

    def wait_group(slot):
        pltpu.make_async_copy(tab_ref.at[pl.ds(0, grp * n_sel)], bufs[slot], sem_ref.at[slot]).wait()

    def start_group_rolled(ids_ref, group, slot):
        for j in range(grp):
            def issue_chunk(c, carry, j=j):
                for kk in range(PEER_ISSUE_UNROLL):
                    start_copy(ids_ref, group * grp + j, c * PEER_ISSUE_UNROLL + kk, slot, j, kk % DMA_PRIORITIES)
                return carry
            lax.fori_loop(0, n_sel // PEER_ISSUE_UNROLL, issue_chunk, 0)

    lane_r = lax.broadcasted_iota(jnp.int32, (rr, width), 1) % rr
    diag = lane_r == lax.broadcasted_iota(jnp.int32, (rr, width), 0)
    col_k = lax.broadcasted_iota(jnp.int32, (n_sel, width), 1) // rr
    expand = (col_k == lax.broadcasted_iota(jnp.int32, (n_sel, width), 0)).astype(_bf16)
    nt = (((1,), (1,)), ((), ()))

    def score_partial(g, slot, j, prefetch):
        base = g * grp
        if prefetch:
            for k in range(n_sel):
                start_copy(idx_ref, base + PEER_AHEAD * grp + j, k, (slot + PEER_AHEAD) % PEER_RING, j,
                           k % DMA_PRIORITIES)
        u = slab_matrix(slot, j, 0)
        prod = lax.dot_general(widen(x_ref[base + j]), u, nt, preferred_element_type=_f32)
        return jnp.sum(jnp.where(diag, prod, 0.0), axis=0, keepdims=True)

    def group_scores(partial):
        p_hi = partial.astype(_bf16)
        p_lo = (partial - p_hi.astype(_f32)).astype(_bf16)
        both = lax.dot_general(jnp.concatenate([p_hi, p_lo], axis=0), expand, nt, preferred_element_type=_f32)
        return both[:grp] + both[grp:]

    def group_activations(scores, g):
        gate = jnp.concatenate([gate_ref[g * grp + j] for j in range(grp)], axis=0)
        act = jax.nn.gelu(scores) * gate
        return jnp.dot(act.astype(_bf16), expand, preferred_element_type=_f32)

    def group_outputs(act_wide, g, slot):
        ys = []
        for j in range(grp):
            v = slab_matrix(slot, j, 2 * rr)
            lhs = jnp.where(diag, act_wide[j:j + 1, :], 0.0).astype(_bf16)
            out = jnp.dot(lhs, v, preferred_element_type=_f32)
            y = h_ref[g * grp + j] + jnp.concatenate([out[:, :LANES], out[:, LANES:]], axis=0)
            if emit_h:
                store_row(o_ref, g * grp + j, y)
            ys.append(y)
        return tuple(ys)

    def finish_group(partial, g, slot):
        return group_outputs(group_activations(group_scores(partial), g), g, slot)

    def store_row(ref, tok, y):
        if len(ref.shape) == 2:
            ref[pl.ds(tok, 1), :] = pltpu.einshape("erl->e(rl)", y[None])
        else:
            ref[tok] = y

    def norm_group(ys, g):
        for j, y in enumerate(ys):
            ms = jnp.sum(jnp.sum(y * y, axis=1, keepdims=True), axis=0, keepdims=True) * (1.0 / (rows * LANES))
            store_row(n_ref, g * grp + j, (y * lax.rsqrt(ms + RMS_EPS) * g_ref[...]).astype(n_ref.dtype))

    def group_step(g, slot, partial_prev, ys_prev2, next_block_group=None):
        wait_group(slot)
        if next_block_group is not None:
            @pl.when(step < last_step)
            def _():
                start_group_rolled(idx_next_ref, next_block_group, (slot + PEER_AHEAD) % PEER_RING)
        prefetch = next_block_group is None
        prev, prev_slot = g - 1, (slot - 1) % PEER_RING
        stages = [] if partial_prev is None else [group_scores, lambda s: group_activations(s, prev)]
        state = partial_prev
        partial = []
        for j in range(grp):
            if j < len(stages):
                state = stages[j](state)
            partial.append(score_partial(g, slot, j, prefetch))
        for stage in stages[grp:]:
            state = stage(state)
        ys = None if partial_prev is None else group_outputs(state, prev, prev_slot)
        if ys_prev2 is not None:
            norm_group(ys_prev2, g - 2)
        return jnp.concatenate(partial, axis=0), ys

    @pl.when(step == 0)
    def _():
        for slot in range(PEER_AHEAD):
            start_group_rolled(idx_ref, slot, slot)

    partial, _ = group_step(0, 0, None, None)
    carry = group_step(1, 1, partial, None)

    def ring_body(it, carry):
        for s in range(2, PEER_RING + 2):
            carry = group_step(it * PEER_RING + s, s % PEER_RING, *carry)
        return carry

    carry = lax.fori_loop(0, n_groups // PEER_RING - 1, ring_body, carry)
    for s in range(2, PEER_RING):
        g = n_groups - PEER_RING + s
        carry = group_step(g, s, *carry, next_block_group=g + PEER_AHEAD - n_groups)
    partial, ys = carry
    norm_group(ys, n_groups - 2)
    norm_group(finish_group(partial, n_groups - 1, PEER_RING - 1), n_groups - 1)


def _peer_experts(idx, gates, xn, h, table, norm_g, norm_dtype, emit_h):
    t, d = xn.shape
    rows = d // LANES
    n_sel = idx.shape[1]
    tb = _blk(t, PEER_TOKENS_PER_STEP)
    n_steps = t // tb
    n_groups = tb // PEER_TOKENS_PER_GROUP
    assert n_groups % PEER_RING == 0 and n_groups >= 2 * PEER_RING
    assert n_sel % PEER_ISSUE_UNROLL == 0 and rows % (2 * BF16_TILE_ROWS) == 0
    slot_shape = (PEER_TOKENS_PER_GROUP * n_sel, 2 * rows, LANES)
    row_spec = pl.BlockSpec((tb, rows, LANES), lambda i: (i, 0, 0))
    flat_spec = pl.BlockSpec((tb, d), lambda i: (i, 0))
    norm_flat = jnp.dtype(norm_dtype).itemsize == 4
    out_specs = [flat_spec if norm_flat else row_spec]
    out_shape = [jax.ShapeDtypeStruct((t, d) if norm_flat else (t, rows, LANES), norm_dtype)]
    if emit_h:
        out_specs.insert(0, flat_spec)
        out_shape.insert(0, jax.ShapeDtypeStruct((t, d), _f32))
    outs = pl.pallas_call(
        functools.partial(_peer_kernel, n_sel=n_sel, emit_h=emit_h),
        grid=(n_steps,),
        in_specs=[
            pl.BlockSpec((tb, n_sel), lambda i: (i, 0), memory_space=pltpu.SMEM),
            pl.BlockSpec((tb, n_sel), lambda i: (jnp.minimum(i + 1, n_steps - 1), 0), memory_space=pltpu.SMEM),
            pl.BlockSpec((tb, 1, n_sel), lambda i: (i, 0, 0)),
            row_spec,
            row_spec,
            pl.BlockSpec((rows, LANES), lambda i: (0, 0)),
            pl.BlockSpec(memory_space=pl.ANY),
        ],
        out_specs=out_specs,
        out_shape=out_shape,
        scratch_shapes=[pltpu.VMEM(slot_shape, _bf16) for _ in range(PEER_RING)]
                       + [pltpu.SemaphoreType.DMA((PEER_RING,))],
        compiler_params=_params("arbitrary"),
        name="peer_experts",
    )(idx, idx, gates.reshape(t, 1, n_sel), xn.reshape(t, rows, LANES), h.reshape(t, rows, LANES),
      norm_g.reshape(rows, LANES), table)
    return (outs[0] if emit_h else None), outs[-1].reshape(t, d)


TABLE_EXPERTS_PER_STEP = 256


def _table_kernel(u_ref, v_ref, o_ref):
    rows = o_ref.shape[1] // 2
    for src, first in ((u_ref, 0), (v_ref, rows)):
        slab = pltpu.einshape("e(rl)->erl", src[...], l=LANES)
        o_ref[:, first:first + rows, :] = slab.astype(o_ref.dtype)


def _expert_table(u_stack, v_stack, layer):
    _, e, d = u_stack.shape
    rows = d // LANES
    eb = _blk(e, TABLE_EXPERTS_PER_STEP)
    src_spec = pl.BlockSpec((None, eb, d), lambda i: (layer, i, 0))
    return pl.pallas_call(
        _table_kernel,
        grid=(e // eb,),
        in_specs=[src_spec, src_spec],
        out_specs=pl.BlockSpec((eb, 2 * rows, LANES), lambda i: (i, 0, 0)),
        out_shape=jax.ShapeDtypeStruct((e, 2 * rows, LANES), _bf16),
        compiler_params=_params("parallel"),
        name="expert_table",
    )(u_stack, v_stack)


def kernel(x, mix_norm, w_in, conv_w, conv_b, w_rg, b_rg, w_ig, b_ig, lru_lambda, w_pool, pool_scale,
           w_out, ffn_norm, w_query, sub_keys, expert_u, expert_v, final_norm):
    batch, seq, d = x.shape
    depth = w_in.shape[0]
    t = batch * seq
    h = x.reshape(t, d)
    xn = _rmsnorm(h, mix_norm[0], _bf16)
    for l in range(depth):
        last = l == depth - 1
        z = _matmul(xn, w_in, l, name="in_proj")
        ypool = _pool_branch(z, w_pool[l].astype(_bf16), pool_scale[l], batch, seq, d)
        merged = _lru_branch(z, ypool, conv_w[l], conv_b[l], w_rg[l].astype(_bf16), b_rg[l],
                             w_ig[l].astype(_bf16), b_ig[l], lru_lambda[l], batch, seq, d)
        h = _matmul(merged, w_out, l, residual=h, name="out_proj")
        xn = _rmsnorm(h, ffn_norm[l], _bf16)
        idx, gates = _route(xn, w_query, l, sub_keys[l].astype(_bf16))
        h, xn = _peer_experts(idx, gates, xn, h, _expert_table(expert_u, expert_v, l),
                              final_norm if last else mix_norm[l + 1], _f32 if last else _bf16, emit_h=not last)
    return xn.reshape(batch, seq, d)
```

```python
import functools

import jax
import jax.numpy as jnp
from jax import lax
from jax.experimental import pallas as pl
from jax.experimental.pallas import tpu as pltpu

RMS_EPS = 1e-6
RG_C = 8.0
CONV_WIDTH = 4
POOL_WINDOWS = (2, 4, 8, 16)
PEER_TOPK = 16

LANES = 128
SUBLANES = 8
VMEM_LIMIT_BYTES = 56 * 1024 * 1024
HALO_ROWS_POOL = 16
HALO_ROWS_CONV = 8
POOL_TIME_BLOCK = 256
LRU_TIME_BLOCK = 512

_f32 = jnp.float32
_bf16 = jnp.bfloat16


def _params(*sem):
    return pltpu.CompilerParams(dimension_semantics=sem, vmem_limit_bytes=VMEM_LIMIT_BYTES)


def _blk(dim, pref):
    return pref if dim % pref == 0 else dim


def _rmsnorm_kernel(x_ref, g_ref, o_ref):
    x = x_ref[...]
    ms = jnp.mean(x * x, axis=-1, keepdims=True)
    o_ref[...] = (x * lax.rsqrt(ms + RMS_EPS) * g_ref[...]).astype(o_ref.dtype)


def _rmsnorm(x, g, out_dtype):
    t, d = x.shape
    tm = _blk(t, 256)
    return pl.pallas_call(
        _rmsnorm_kernel,
        grid=(t // tm,),
        in_specs=[pl.BlockSpec((tm, d), lambda i: (i, 0)),
                  pl.BlockSpec((1, d), lambda i: (0, 0))],
        out_specs=pl.BlockSpec((tm, d), lambda i: (i, 0)),
        out_shape=jax.ShapeDtypeStruct((t, d), out_dtype),
        compiler_params=_params("parallel"),
        name="rmsnorm",
    )(x, g.reshape(1, d))


def _matmul_kernel(a_ref, w_ref, o_ref):
    w = w_ref[...].astype(_bf16)
    o_ref[...] = jnp.dot(a_ref[...], w, preferred_element_type=_f32).astype(o_ref.dtype)


def _matmul_res_kernel(a_ref, w_ref, r_ref, o_ref):
    w = w_ref[...].astype(_bf16)
    o_ref[...] = r_ref[...] + jnp.dot(a_ref[...], w, preferred_element_type=_f32)


def _matmul(a, w_stack, layer, residual=None, out_dtype=_f32, name="matmul"):
    m, k = a.shape
    n = w_stack.shape[2]
    tm = _blk(m, 1024)
    tn = _blk(n, 512)
    in_specs = [pl.BlockSpec((tm, k), lambda i, j: (i, 0)),
                pl.BlockSpec((None, k, tn), lambda i, j: (layer, 0, j))]
    w = w_stack
    args = [a, w]
    kern = _matmul_kernel
    if residual is not None:
        in_specs.append(pl.BlockSpec((tm, tn), lambda i, j: (i, j)))
        args.append(residual)
        kern = _matmul_res_kernel
    return pl.pallas_call(
        kern,
        grid=(m // tm, n // tn),
        in_specs=in_specs,
        out_specs=pl.BlockSpec((tm, tn), lambda i, j: (i, j)),
        out_shape=jax.ShapeDtypeStruct((m, n), out_dtype),
        compiler_params=_params("parallel", "arbitrary"),
        name=name,
    )(*args)


def _pool_kernel(x_ref, halo_ref, gate_ref, w_ref, scale_ref, o_ref, *, group_dim):
    t = pl.program_id(1)
    ts = x_ref.shape[0]
    row = lax.broadcasted_iota(jnp.int32, (ts, 1), 0) + t * ts
    first = t == 0
    for g, win in enumerate(POOL_WINDOWS):
        cols = slice(g * group_dim, (g + 1) * group_dim)
        x = x_ref[:, cols]
        halo = jnp.where(first, 0.0, halo_ref[:, cols])
        acc = jnp.concatenate([halo, x], axis=0)
        step = 1
        while step < win:
            acc = acc + pltpu.roll(acc, step, axis=0)
            step *= 2
        wsum = acc[HALO_ROWS_POOL:]
        count = jnp.minimum(row + 1, win).astype(_f32)
        pooled = wsum / count - x
        y = jnp.dot(pooled.astype(_bf16), w_ref[g], preferred_element_type=_f32)
        y = y * scale_ref[:, cols]
        o_ref[:, cols] = (jax.nn.sigmoid(gate_ref[:, cols]) * y).astype(o_ref.dtype)


def _pool_branch(z, w_pool, pool_scale, batch, seq, d):
    t = batch * seq
    groups, group_dim = w_pool.shape[0], w_pool.shape[1]
    ts = _blk(seq, POOL_TIME_BLOCK)
    nt = seq // ts
    hb = ts // HALO_ROWS_POOL
    x_col, gate_col = 2, 4
    return pl.pallas_call(
        functools.partial(_pool_kernel, group_dim=group_dim),
        grid=(batch, nt),
        in_specs=[
            pl.BlockSpec((ts, d), lambda b, s: (b * nt + s, x_col)),
            pl.BlockSpec((HALO_ROWS_POOL, d),
                         lambda b, s: (jnp.maximum((b * nt + s) * hb - 1, 0), x_col)),
            pl.BlockSpec((ts, d), lambda b, s: (b * nt + s, gate_col)),
            pl.BlockSpec((groups, group_dim, group_dim), lambda b, s: (0, 0, 0)),
            pl.BlockSpec((1, d), lambda b, s: (0, 0)),
        ],
        out_specs=pl.BlockSpec((ts, d), lambda b, s: (b * nt + s, 0)),
        out_shape=jax.ShapeDtypeStruct((t, d), _bf16),
        compiler_params=_params("parallel", "arbitrary"),
        name="pool_branch",
    )(z, z, z, w_pool, pool_scale.reshape(1, d))


def _lru_kernel(x_ref, halo_ref, gelu_ref, glru_ref, ypool_ref, convw_ref, convb_ref,
                wrg_ref, brg_ref, wig_ref, big_ref, lam_ref, o_ref, carry_ref):
    t = pl.program_id(2)
    ts = x_ref.shape[0]

    @pl.when(t == 0)
    def _():
        carry_ref[...] = jnp.zeros_like(carry_ref)

    x = x_ref[...]
    halo = jnp.where(t == 0, 0.0, halo_ref[...])
    ext = jnp.concatenate([halo, x], axis=0)
    conv = convb_ref[...] + x * convw_ref[CONV_WIDTH - 1:CONV_WIDTH, :]
    for k in range(CONV_WIDTH - 1):
        shifted = pltpu.roll(ext, CONV_WIDTH - 1 - k, axis=0)[HALO_ROWS_CONV:]
        conv = conv + shifted * convw_ref[k:k + 1, :]

    cb = conv.astype(_bf16)
    r = jax.nn.sigmoid(jnp.dot(cb, wrg_ref[0], preferred_element_type=_f32) + brg_ref[...])
    i = jax.nn.sigmoid(jnp.dot(cb, wig_ref[0], preferred_element_type=_f32) + big_ref[...])
    neg_lam = -lam_ref[...]
    softplus = jnp.maximum(neg_lam, 0.0) + jnp.log(1.0 + jnp.exp(-jnp.abs(neg_lam)))
    log_a = -RG_C * r * softplus
    a = jnp.exp(log_a)
    u = jnp.sqrt(1.0 - a * a) * (i * conv)

    tiles = ts // SUBLANES
    a = a.reshape(tiles, SUBLANES, a.shape[-1])
    u = u.reshape(tiles, SUBLANES, u.shape[-1])
    sub = lax.broadcasted_iota(jnp.int32, a.shape, 1)
    step = 1
    while step < SUBLANES:
        keep = sub >= step
        a_prev = jnp.where(keep, pltpu.roll(a, step, axis=1), 1.0)
        u_prev = jnp.where(keep, pltpu.roll(u, step, axis=1), 0.0)
        u = a * u_prev + u
        a = a * a_prev
        step *= 2
    carry = carry_ref[...]
    hs = []
    for i in range(tiles):
        hs.append(a[i] * carry + u[i])
        carry = hs[-1][SUBLANES - 1:SUBLANES, :]
    carry_ref[...] = carry
    h = jnp.concatenate(hs, axis=0)

    y = h * jax.nn.gelu(gelu_ref[...])
    merged = jax.nn.sigmoid(glru_ref[...]) * y + ypool_ref[...].astype(_f32)
    o_ref[...] = merged.astype(o_ref.dtype)


def _lru_branch(z, ypool, conv_w, conv_b, w_rg, b_rg, w_ig, b_ig, lam, batch, seq, d):
    t = batch * seq
    heads, hd = w_rg.shape[0], w_rg.shape[1]
    ts = _blk(seq, LRU_TIME_BLOCK)
    nt = seq // ts
    hb = ts // HALO_ROWS_CONV
    nh = d // hd
    row = lambda b, h, s: b * nt + s
    vec = pl.BlockSpec((1, hd), lambda b, h, s: (0, h))
    return pl.pallas_call(
        _lru_kernel,
        grid=(batch, heads, nt),
        in_specs=[
            pl.BlockSpec((ts, hd), lambda b, h, s: (row(b, h, s), h)),
            pl.BlockSpec((HALO_ROWS_CONV, hd),
                         lambda b, h, s: (jnp.maximum(row(b, h, s) * hb - 1, 0), h)),
            pl.BlockSpec((ts, hd), lambda b, h, s: (row(b, h, s), nh + h)),
            pl.BlockSpec((ts, hd), lambda b, h, s: (row(b, h, s), 3 * nh + h)),
            pl.BlockSpec((ts, hd), lambda b, h, s: (row(b, h, s), h)),
            pl.BlockSpec((CONV_WIDTH, hd), lambda b, h, s: (0, h)),
            vec,
            pl.BlockSpec((1, hd, hd), lambda b, h, s: (h, 0, 0)),
            vec,
            pl.BlockSpec((1, hd, hd), lambda b, h, s: (h, 0, 0)),
            vec,
            vec,
        ],
        out_specs=pl.BlockSpec((ts, hd), lambda b, h, s: (row(b, h, s), h)),
        out_shape=jax.ShapeDtypeStruct((t, d), _bf16),
        scratch_shapes=[pltpu.VMEM((1, hd), _f32)],
        compiler_params=_params("parallel", "parallel", "arbitrary"),
        name="lru_branch",
    )(z, z, z, z, ypool, conv_w, conv_b.reshape(1, d), w_rg, b_rg.reshape(1, d),
      w_ig, b_ig.reshape(1, d), lam.reshape(1, d))


def _topk_rows(s, k, ids=None):
    if ids is None:
        ids = lax.broadcasted_iota(jnp.int32, s.shape, 0).astype(_f32)
    big = float(2 ** 20)
    vals, idxs = [], []
    for _ in range(k):
        m = jnp.max(s, axis=0, keepdims=True)
        idx = jnp.min(jnp.where(s == m, ids, big), axis=0, keepdims=True)
        vals.append(m)
        idxs.append(idx)
        s = jnp.where(ids == idx, -jnp.inf, s)
    return jnp.concatenate(vals, axis=0), jnp.concatenate(idxs, axis=0)


def _product_candidates(v1, v2):
    k = PEER_TOPK
    half = k // 2
    n = v1.shape[1]
    sub = lax.broadcasted_iota(jnp.int32, (half, n), 0)
    blocks = [v1[0:1] + v2]
    ids = [lax.broadcasted_iota(jnp.int32, (k, n), 0)]
    blocks.append(v1[1:2] + v2[0:half])
    ids.append(k + sub)
    blocks.append(v1[half:k] + v2[0:1])
    ids.append((sub + half) * k)
    for b in range(k // 3):
        ok = (sub >= 2) & ((sub + 1) * (b + 1) <= k)
        blocks.append(jnp.where(ok, v1[0:half] + v2[b:b + 1], -jnp.inf))
        ids.append(sub * k + b)
    return jnp.concatenate(blocks, axis=0), jnp.concatenate(ids, axis=0).astype(_f32)


def _route_kernel(x_ref, wq_ref, keys_ref, idx_ref, gate_ref, wq_bf16_ref, *, n_keys):
    half = keys_ref.shape[-1]
    @pl.when(pl.program_id(1) == 0)
    def _():
        wq_bf16_ref[...] = wq_ref[...].astype(_bf16)

    q = jnp.dot(x_ref[...], wq_bf16_ref[...], preferred_element_type=_f32).astype(_bf16)
    nt = (((1,), (1,)), ((), ()))
    s1 = lax.dot_general(keys_ref[0, 0], q[:, :half], nt, preferred_element_type=_f32)
    s2 = lax.dot_general(keys_ref[0, 1], q[:, half:], nt, preferred_element_type=_f32)
    v1, i1 = _topk_rows(s1, PEER_TOPK)
    v2, i2 = _topk_rows(s2, PEER_TOPK)
    cand, cand_ids = _product_candidates(v1, v2)
    vals, pos = _topk_rows(cand, PEER_TOPK, cand_ids)
    pos_hi = jnp.floor(pos * (1.0 / PEER_TOPK))
    pos_lo = pos - pos_hi * PEER_TOPK
    e1 = jnp.zeros_like(pos)
    e2 = jnp.zeros_like(pos)
    for a in range(PEER_TOPK):
        e1 = e1 + jnp.where(pos_hi == a, i1[a:a + 1], 0.0)
        e2 = e2 + jnp.where(pos_lo == a, i2[a:a + 1], 0.0)
    idx_ref[0] = (e1 * n_keys + e2).astype(jnp.int32)
    ex = jnp.exp(vals - vals[0:1])
    gate_ref[0] = ex / jnp.sum(ex, axis=0, keepdims=True)


def _route(xn, w_query_stack, layer, sub_keys):
    t, d = xn.shape
    heads, _, n_keys, half = sub_keys.shape
    qd = 2 * half
    tm = _blk(t, 256)
    out_sds = lambda dt: jax.ShapeDtypeStruct((heads, PEER_TOPK, t), dt)
    idx, gates = pl.pallas_call(
        functools.partial(_route_kernel, n_keys=n_keys),
        grid=(heads, t // tm),
        in_specs=[pl.BlockSpec((tm, d), lambda h, i: (i, 0)),
                  pl.BlockSpec((None, d, qd), lambda h, i: (layer, 0, h)),
                  pl.BlockSpec((1, 2, n_keys, half), lambda h, i: (h, 0, 0, 0))],
        out_specs=[pl.BlockSpec((1, PEER_TOPK, tm), lambda h, i: (h, 0, i)),
                   pl.BlockSpec((1, PEER_TOPK, tm), lambda h, i: (h, 0, i))],
        out_shape=[out_sds(jnp.int32), out_sds(_f32)],
        scratch_shapes=[pltpu.VMEM((d, qd), _bf16)],
        compiler_params=_params("parallel", "arbitrary"),
        name="peer_route",
    )(xn, w_query_stack, sub_keys)
    flat = lambda a: jnp.transpose(a, (2, 0, 1)).reshape(t, heads * PEER_TOPK)
    return flat(idx), flat(gates)


PEER_TOKENS_PER_STEP = 128
PEER_TOKENS_PER_GROUP = 2
PEER_RING = 8
PEER_AHEAD = 6
PEER_ISSUE_UNROLL = 8
BF16_TILE_ROWS = 16
DMA_PRIORITIES = 2


def _peer_kernel(idx_ref, idx_next_ref, gate_ref, x_ref, h_ref, g_ref, tab_ref, *rest, n_sel, emit_h):
    n_out = 2 if emit_h else 1
    outs, scratch = rest[:n_out], rest[n_out:]
    o_ref = outs[0] if emit_h else None
    n_ref = outs[-1]
    bufs, sem_ref = scratch[:PEER_RING], scratch[PEER_RING]
    tb, rows, _ = x_ref.shape
    rr = rows // 2
    grp = PEER_TOKENS_PER_GROUP
    n_groups = tb // grp
    width = n_sel * rr
    step = pl.program_id(0)
    last_step = pl.num_programs(0) - 1

    def widen(a):
        return jnp.concatenate([a[..., 0:rr, :], a[..., rr:2 * rr, :]], axis=-1)

    def slab_matrix(slot, j, first_row):
        part = lambda r0: bufs[slot][pl.ds(j * n_sel, n_sel), r0:r0 + rr, :].reshape(width, LANES)
        return jnp.concatenate([part(first_row), part(first_row + rr)], axis=-1)

    def start_copy(ids_ref, tok, k, slot, j, priority):
        e = ids_ref[tok, k]
        pltpu.make_async_copy(tab_ref.at[e], bufs[slot].at[j * n_sel + k], sem_ref.at[slot]).start(priority=priority)

    def wait_group(slot):
        pltpu.make_async_copy(tab_ref.at[pl.ds(0, grp * n_sel)], bufs[slot], sem_ref.at[slot]).wait()

    def start_group_rolled(ids_ref, group, slot):
        for j in range(grp):
            def issue_chunk(c, carry, j=j):
                for kk in range(PEER_ISSUE_UNROLL):
                    start_copy(ids_ref, group * grp + j, c * PEER_ISSUE_UNROLL + kk, slot, j, kk % DMA_PRIORITIES)
                return carry
            lax.fori_loop(0, n_sel // PEER_ISSUE_UNROLL, issue_chunk, 0)

    lane_r = lax.broadcasted_iota(jnp.int32, (rr, width), 1) % rr
    diag = lane_r == lax.broadcasted_iota(jnp.int32, (rr, width), 0)
    col_k = lax.broadcasted_iota(jnp.int32, (n_sel, width), 1) // rr
    expand = (col_k == lax.broadcasted_iota(jnp.int32, (n_sel, width), 0)).astype(_bf16)
    nt = (((1,), (1,)), ((), ()))

    def score_partial(g, slot, j, prefetch):
        base = g * grp
        if prefetch:
            for k in range(n_sel):
                start_copy(idx_ref, base + PEER_AHEAD * grp + j, k, (slot + PEER_AHEAD) % PEER_RING, j,
                           k % DMA_PRIORITIES)
        u = slab_matrix(slot, j, 0)
        prod = lax.dot_general(widen(x_ref[base + j]), u, nt, preferred_element_type=_f32)
        return jnp.sum(jnp.where(diag, prod, 0.0), axis=0, keepdims=True)

    def group_scores(partial):
        p_hi = partial.astype(_bf16)
        p_lo = (partial - p_hi.astype(_f32)).astype(_bf16)
        both = lax.dot_general(jnp.concatenate([p_hi, p_lo], axis=0), expand, nt, preferred_element_type=_f32)
        return both[:grp] + both[grp:]

    def group_activations(scores, g):
        gate = jnp.concatenate([gate_ref[g * grp + j] for j in range(grp)], axis=0)
        act = jax.nn.gelu(scores) * gate
        return jnp.dot(act.astype(_bf16), expand, preferred_element_type=_f32)

    def group_outputs(act_wide, g, slot):
        ys = []
        for j in range(grp):
            v = slab_matrix(slot, j, 2 * rr)
            lhs = jnp.where(diag, act_wide[j:j + 1, :], 0.0).astype(_bf16)
            out = jnp.dot(lhs, v, preferred_element_type=_f32)
            y = h_ref[g * grp + j] + jnp.concatenate([out[:, :LANES], out[:, LANES:]], axis=0)
            if emit_h:
                store_row(o_ref, g * grp + j, y)
            ys.append(y)
        return tuple(ys)

    def finish_group(partial, g, slot):
        return group_outputs(group_activations(group_scores(partial), g), g, slot)

    def store_row(ref, tok, y):
        if len(ref.shape) == 2:
            ref[pl.ds(tok, 1), :] = pltpu.einshape("erl->e(rl)", y[None])
        else:
            ref[tok] = y

    def norm_group(ys, g):
        for j, y in enumerate(ys):
            ms = jnp.sum(jnp.sum(y * y, axis=1, keepdims=True), axis=0, keepdims=True) * (1.0 / (rows * LANES))
            store_row(n_ref, g * grp + j, (y * lax.rsqrt(ms + RMS_EPS) * g_ref[...]).astype(n_ref.dtype))

    def group_step(g, slot, partial_prev, ys_prev2, next_block_group=None):
        wait_group(slot)
        if next_block_group is not None:
            @pl.when(step < last_step)
            def _():
                start_group_rolled(idx_next_ref, next_block_group, (slot + PEER_AHEAD) % PEER_RING)
        prefetch = next_block_group is None
        prev, prev_slot = g - 1, (slot - 1) % PEER_RING
        stages = [] if partial_prev is None else [group_scores, lambda s: group_activations(s, prev)]
        state = partial_prev
        partial = []
        for j in range(grp):
            if j < len(stages):
                state = stages[j](state)
            partial.append(score_partial(g, slot, j, prefetch))
        for stage in stages[grp:]:
            state = stage(state)
        ys = None if partial_prev is None else group_outputs(state, prev, prev_slot)
        if ys_prev2 is not None:
            norm_group(ys_prev2, g - 2)
        return jnp.concatenate(partial, axis=0), ys

    @pl.when(step == 0)
    def _():
        for slot in range(PEER_AHEAD):
            start_group_rolled(idx_ref, slot, slot)

    partial, _ = group_step(0, 0, None, None)
    carry = group_step(1, 1, partial, None)

    def ring_body(it, carry):
        for s in range(2, PEER_RING + 2):
            carry = group_step(it * PEER_RING + s, s % PEER_RING, *carry)
        return carry

    carry = lax.fori_loop(0, n_groups // PEER_RING - 1, ring_body, carry)
    for s in range(2, PEER_RING):
        g = n_groups - PEER_RING + s
        carry = group_step(g, s, *carry, next_block_group=g + PEER_AHEAD - n_groups)
    partial, ys = carry
    norm_group(ys, n_groups - 2)
    norm_group(finish_group(partial, n_groups - 1, PEER_RING - 1), n_groups - 1)


def _peer_experts(idx, gates, xn, h, table, norm_g, norm_dtype, emit_h):
    t, d = xn.shape
    rows = d // LANES
    n_sel = idx.shape[1]
    tb = _blk(t, PEER_TOKENS_PER_STEP)
    n_steps = t // tb
    n_groups = tb // PEER_TOKENS_PER_GROUP
    assert n_groups % PEER_RING == 0 and n_groups >= 2 * PEER_RING
    assert n_sel % PEER_ISSUE_UNROLL == 0 and rows % (2 * BF16_TILE_ROWS) == 0
    slot_shape = (PEER_TOKENS_PER_GROUP * n_sel, 2 * rows, LANES)
    row_spec = pl.BlockSpec((tb, rows, LANES), lambda i: (i, 0, 0))
    flat_spec = pl.BlockSpec((tb, d), lambda i: (i, 0))
    norm_flat = jnp.dtype(norm_dtype).itemsize == 4
    out_specs = [flat_spec if norm_flat else row_spec]
    out_shape = [jax.ShapeDtypeStruct((t, d) if norm_flat else (t, rows, LANES), norm_dtype)]
    if emit_h:
        out_specs.insert(0, flat_spec)
        out_shape.insert(0, jax.ShapeDtypeStruct((t, d), _f32))
    outs = pl.pallas_call(
        functools.partial(_peer_kernel, n_sel=n_sel, emit_h=emit_h),
        grid=(n_steps,),
        in_specs=[
            pl.BlockSpec((tb, n_sel), lambda i: (i, 0), memory_space=pltpu.SMEM),
            pl.BlockSpec((tb, n_sel), lambda i: (jnp.minimum(i + 1, n_steps - 1), 0), memory_space=pltpu.SMEM),
            pl.BlockSpec((tb, 1, n_sel), lambda i: (i, 0, 0)),
            row_spec,
            row_spec,
            pl.BlockSpec((rows, LANES), lambda i: (0, 0)),
            pl.BlockSpec(memory_space=pl.ANY),
        ],
        out_specs=out_specs,
        out_shape=out_shape,
        scratch_shapes=[pltpu.VMEM(slot_shape, _bf16) for _ in range(PEER_RING)]
                       + [pltpu.SemaphoreType.DMA((PEER_RING,))],
        compiler_params=_params("arbitrary"),
        name="peer_experts",
    )(idx, idx, gates.reshape(t, 1, n_sel), xn.reshape(t, rows, LANES), h.reshape(t, rows, LANES),
      norm_g.reshape(rows, LANES), table)
    return (outs[0] if emit_h else None), outs[-1].reshape(t, d)


TABLE_EXPERTS_PER_STEP = 256


def _table_kernel(u_ref, v_ref, o_ref):
    rows = o_ref.shape[1] // 2
    for src, first in ((u_ref, 0), (v_ref, rows)):
        slab = pltpu.einshape("e(rl)->erl", src[...], l=LANES)
        o_ref[:, first:first + rows, :] = slab.astype(o_ref.dtype)


def _expert_table(u_stack, v_stack, layer):
    _, e, d = u_stack.shape
    rows = d // LANES
    eb = _blk(e, TABLE_EXPERTS_PER_STEP)
    src_spec = pl.BlockSpec((None, eb, d), lambda i: (layer, i, 0))
    return pl.pallas_call(
        _table_kernel,
        grid=(e // eb,),
        in_specs=[src_spec, src_spec],
        out_specs=pl.BlockSpec((eb, 2 * rows, LANES), lambda i: (i, 0, 0)),
        out_shape=jax.ShapeDtypeStruct((e, 2 * rows, LANES), _bf16),
        compiler_params=_params("parallel"),
        name="expert_table",
    )(u_stack, v_stack)


def kernel(x, mix_norm, w_in, conv_w, conv_b, w_rg, b_rg, w_ig, b_ig, lru_lambda, w_pool, pool_scale,
           w_out, ffn_norm, w_query, sub_keys, expert_u, expert_v, final_norm):
    batch, seq, d = x.shape
    depth = w_in.shape[0]
    t = batch * seq
    h = x.reshape(t, d)
    xn = _rmsnorm(h, mix_norm[0], _bf16)
    for l in range(depth):
        last = l == depth - 1
        z = _matmul(xn, w_in, l, name="in_proj")
        ypool = _pool_branch(z, w_pool[l].astype(_bf16), pool_scale[l], batch, seq, d)
        merged = _lru_branch(z, ypool, conv_w[l], conv_b[l], w_rg[l].astype(_bf16), b_rg[l],
                             w_ig[l].astype(_bf16), b_ig[l], lru_lambda[l], batch, seq, d)
        h = _matmul(merged, w_out, l, residual=h, name="out_proj")
        xn = _rmsnorm(h, ffn_norm[l], _bf16)
        idx, gates = _route(xn, w_query, l, sub_keys[l].astype(_bf16))
        h, xn = _peer_experts(idx, gates, xn, h, _expert_table(expert_u, expert_v, l),
                              final_norm if last else mix_norm[l + 1], _f32 if last else _bf16, emit_h=not last)
    return xn.reshape(batch, seq, d)
```

```python
import functools

import jax
import jax.numpy as jnp
from jax import lax
from jax.experimental import pallas as pl
from jax.experimental.pallas import tpu as pltpu

RMS_EPS = 1e-6
RG_C = 8.0
CONV_WIDTH = 4
POOL_WINDOWS = (2, 4, 8, 16)
PEER_TOPK = 16

LANES = 128
SUBLANES = 8
VMEM_LIMIT_BYTES = 56 * 1024 * 1024
HALO_ROWS_POOL = 16
HALO_ROWS_CONV = 8
POOL_TIME_BLOCK = 256
LRU_TIME_BLOCK = 1024

_f32 = jnp.float32
_bf16 = jnp.bfloat16


def _params(*sem):
    return pltpu.CompilerParams(dimension_semantics=sem, vmem_limit_bytes=VMEM_LIMIT_BYTES)


def _blk(dim, pref):
    return pref if dim % pref == 0 else dim


def _rmsnorm_kernel(x_ref, g_ref, o_ref):
    x = x_ref[...]
    ms = jnp.mean(x * x, axis=-1, keepdims=True)
    o_ref[...] = (x * lax.rsqrt(ms + RMS_EPS) * g_ref[...]).astype(o_ref.dtype)


def _rmsnorm(x, g, out_dtype):
    t, d = x.shape
    tm = _blk(t, 256)
    return pl.pallas_call(
        _rmsnorm_kernel,
        grid=(t // tm,),
        in_specs=[pl.BlockSpec((tm, d), lambda i: (i, 0)),
                  pl.BlockSpec((1, d), lambda i: (0, 0))],
        out_specs=pl.BlockSpec((tm, d), lambda i: (i, 0)),
        out_shape=jax.ShapeDtypeStruct((t, d), out_dtype),
        compiler_params=_params("parallel"),
        name="rmsnorm",
    )(x, g.reshape(1, d))


def _matmul_kernel(a_ref, w_ref, o_ref):
    w = w_ref[...].astype(_bf16)
    o_ref[...] = jnp.dot(a_ref[...], w, preferred_element_type=_f32).astype(o_ref.dtype)


def _matmul_res_kernel(a_ref, w_ref, r_ref, o_ref):
    w = w_ref[...].astype(_bf16)
    o_ref[...] = r_ref[...] + jnp.dot(a_ref[...], w, preferred_element_type=_f32)


def _matmul(a, w_stack, layer, residual=None, out_dtype=_f32, name="matmul"):
    m, k = a.shape
    n = w_stack.shape[2]
    tm = _blk(m, 1024)
    tn = _blk(n, 512)
    in_specs = [pl.BlockSpec((tm, k), lambda i, j: (i, 0)),
                pl.BlockSpec((None, k, tn), lambda i, j: (layer, 0, j))]
    w = w_stack
    args = [a, w]
    kern = _matmul_kernel
    if residual is not None:
        in_specs.append(pl.BlockSpec((tm, tn), lambda i, j: (i, j)))
        args.append(residual)
        kern = _matmul_res_kernel
    return pl.pallas_call(
        kern,
        grid=(m // tm, n // tn),
        in_specs=in_specs,
        out_specs=pl.BlockSpec((tm, tn), lambda i, j: (i, j)),
        out_shape=jax.ShapeDtypeStruct((m, n), out_dtype),
        compiler_params=_params("parallel", "arbitrary"),
        name=name,
    )(*args)


def _pool_kernel(x_ref, halo_ref, gate_ref, w_ref, scale_ref, o_ref, *, group_dim):
    t = pl.program_id(1)
    ts = x_ref.shape[0]
    row = lax.broadcasted_iota(jnp.int32, (ts, 1), 0) + t * ts
    first = t == 0
    for g, win in enumerate(POOL_WINDOWS):
        cols = slice(g * group_dim, (g + 1) * group_dim)
        x = x_ref[:, cols]
        halo = jnp.where(first, 0.0, halo_ref[:, cols])
        acc = jnp.concatenate([halo, x], axis=0)
        step = 1
        while step < win:
            acc = acc + pltpu.roll(acc, step, axis=0)
            step *= 2
        wsum = acc[HALO_ROWS_POOL:]
        count = jnp.minimum(row + 1, win).astype(_f32)
        pooled = wsum / count - x
        y = jnp.dot(pooled.astype(_bf16), w_ref[g], preferred_element_type=_f32)
        y = y * scale_ref[:, cols]
        o_ref[:, cols] = (jax.nn.sigmoid(gate_ref[:, cols]) * y).astype(o_ref.dtype)


def _pool_branch(z, w_pool, pool_scale, batch, seq, d):
    t = batch * seq
    groups, group_dim = w_pool.shape[0], w_pool.shape[1]
    ts = _blk(seq, POOL_TIME_BLOCK)
    nt = seq // ts
    hb = ts // HALO_ROWS_POOL
    x_col, gate_col = 2, 4
    return pl.pallas_call(
        functools.partial(_pool_kernel, group_dim=group_dim),
        grid=(batch, nt),
        in_specs=[
            pl.BlockSpec((ts, d), lambda b, s: (b * nt + s, x_col)),
            pl.BlockSpec((HALO_ROWS_POOL, d),
                         lambda b, s: (jnp.maximum((b * nt + s) * hb - 1, 0), x_col)),
            pl.BlockSpec((ts, d), lambda b, s: (b * nt + s, gate_col)),
            pl.BlockSpec((groups, group_dim, group_dim), lambda b, s: (0, 0, 0)),
            pl.BlockSpec((1, d), lambda b, s: (0, 0)),
        ],
        out_specs=pl.BlockSpec((ts, d), lambda b, s: (b * nt + s, 0)),
        out_shape=jax.ShapeDtypeStruct((t, d), _bf16),
        compiler_params=_params("parallel", "arbitrary"),
        name="pool_branch",
    )(z, z, z, w_pool, pool_scale.reshape(1, d))


def _lru_kernel(x_ref, halo_ref, gelu_ref, glru_ref, ypool_ref, convw_ref, convb_ref,
                wrg_ref, brg_ref, wig_ref, big_ref, lam_ref, o_ref, carry_ref):
    t = pl.program_id(2)
    ts = x_ref.shape[0]

    @pl.when(t == 0)
    def _():
        carry_ref[...] = jnp.zeros_like(carry_ref)

    x = x_ref[...]
    halo = jnp.where(t == 0, 0.0, halo_ref[...])
    ext = jnp.concatenate([halo, x], axis=0)
    conv = convb_ref[...] + x * convw_ref[CONV_WIDTH - 1:CONV_WIDTH, :]
    for k in range(CONV_WIDTH - 1):
        shifted = pltpu.roll(ext, CONV_WIDTH - 1 - k, axis=0)[HALO_ROWS_CONV:]
        conv = conv + shifted * convw_ref[k:k + 1, :]

    cb = conv.astype(_bf16)
    r = jax.nn.sigmoid(jnp.dot(cb, wrg_ref[0], preferred_element_type=_f32) + brg_ref[...])
    i = jax.nn.sigmoid(jnp.dot(cb, wig_ref[0], preferred_element_type=_f32) + big_ref[...])
    neg_lam = -lam_ref[...]
    softplus = jnp.maximum(neg_lam, 0.0) + jnp.log(1.0 + jnp.exp(-jnp.abs(neg_lam)))
    log_a = -RG_C * r * softplus
    a = jnp.exp(log_a)
    u = jnp.sqrt(1.0 - a * a) * (i * conv)

    tiles = ts // SUBLANES
    a = a.reshape(tiles, SUBLANES, a.shape[-1])
    u = u.reshape(tiles, SUBLANES, u.shape[-1])
    sub = lax.broadcasted_iota(jnp.int32, a.shape, 1)
    step = 1
    while step < SUBLANES:
        keep = sub >= step
        a_prev = jnp.where(keep, pltpu.roll(a, step, axis=1), 1.0)
        u_prev = jnp.where(keep, pltpu.roll(u, step, axis=1), 0.0)
        u = a * u_prev + u
        a = a * a_prev
        step *= 2
    carry = carry_ref[...]
    hs = []
    for i in range(tiles):
        hs.append(a[i] * carry + u[i])
        carry = hs[-1][SUBLANES - 1:SUBLANES, :]
    carry_ref[...] = carry
    h = jnp.concatenate(hs, axis=0)

    y = h * jax.nn.gelu(gelu_ref[...])
    merged = jax.nn.sigmoid(glru_ref[...]) * y + ypool_ref[...].astype(_f32)
    o_ref[...] = merged.astype(o_ref.dtype)


def _lru_branch(z, ypool, conv_w, conv_b, w_rg, b_rg, w_ig, b_ig, lam, batch, seq, d):
    t = batch * seq
    heads, hd = w_rg.shape[0], w_rg.shape[1]
    ts = _blk(seq, LRU_TIME_BLOCK)
    nt = seq // ts
    hb = ts // HALO_ROWS_CONV
    nh = d // hd
    row = lambda b, h, s: b * nt + s
    vec = pl.BlockSpec((1, hd), lambda b, h, s: (0, h))
    return pl.pallas_call(
        _lru_kernel,
        grid=(batch, heads, nt),
        in_specs=[
            pl.BlockSpec((ts, hd), lambda b, h, s: (row(b, h, s), h)),
            pl.BlockSpec((HALO_ROWS_CONV, hd),
                         lambda b, h, s: (jnp.maximum(row(b, h, s) * hb - 1, 0), h)),
            pl.BlockSpec((ts, hd), lambda b, h, s: (row(b, h, s), nh + h)),
            pl.BlockSpec((ts, hd), lambda b, h, s: (row(b, h, s), 3 * nh + h)),
            pl.BlockSpec((ts, hd), lambda b, h, s: (row(b, h, s), h)),
            pl.BlockSpec((CONV_WIDTH, hd), lambda b, h, s: (0, h)),
            vec,
            pl.BlockSpec((1, hd, hd), lambda b, h, s: (h, 0, 0)),
            vec,
            pl.BlockSpec((1, hd, hd), lambda b, h, s: (h, 0, 0)),
            vec,
            vec,
        ],
        out_specs=pl.BlockSpec((ts, hd), lambda b, h, s: (row(b, h, s), h)),
        out_shape=jax.ShapeDtypeStruct((t, d), _bf16),
        scratch_shapes=[pltpu.VMEM((1, hd), _f32)],
        compiler_params=_params("parallel", "parallel", "arbitrary"),
        name="lru_branch",
    )(z, z, z, z, ypool, conv_w, conv_b.reshape(1, d), w_rg, b_rg.reshape(1, d),
      w_ig, b_ig.reshape(1, d), lam.reshape(1, d))


def _topk_rows(s, k, ids=None):
    if ids is None:
        ids = lax.broadcasted_iota(jnp.int32, s.shape, 0).astype(_f32)
    big = float(2 ** 20)
    vals, idxs = [], []
    for _ in range(k):
        m = jnp.max(s, axis=0, keepdims=True)
        idx = jnp.min(jnp.where(s == m, ids, big), axis=0, keepdims=True)
        vals.append(m)
        idxs.append(idx)
        s = jnp.where(ids == idx, -jnp.inf, s)
    return jnp.concatenate(vals, axis=0), jnp.concatenate(idxs, axis=0)


def _product_candidates(v1, v2):
    k = PEER_TOPK
    half = k // 2
    n = v1.shape[1]
    sub = lax.broadcasted_iota(jnp.int32, (half, n), 0)
    blocks = [v1[0:1] + v2]
    ids = [lax.broadcasted_iota(jnp.int32, (k, n), 0)]
    blocks.append(v1[1:2] + v2[0:half])
    ids.append(k + sub)
    blocks.append(v1[half:k] + v2[0:1])
    ids.append((sub + half) * k)
    for b in range(k // 3):
        ok = (sub >= 2) & ((sub + 1) * (b + 1) <= k)
        blocks.append(jnp.where(ok, v1[0:half] + v2[b:b + 1], -jnp.inf))
        ids.append(sub * k + b)
    return jnp.concatenate(blocks, axis=0), jnp.concatenate(ids, axis=0).astype(_f32)


def _route_kernel(x_ref, wq_ref, keys_ref, idx_ref, gate_ref, wq_bf16_ref, *, n_keys):
    half = keys_ref.shape[-1]
    @pl.when(pl.program_id(1) == 0)
    def _():
        wq_bf16_ref[...] = wq_ref[...].astype(_bf16)

    q = jnp.dot(x_ref[...], wq_bf16_ref[...], preferred_element_type=_f32).astype(_bf16)
    nt = (((1,), (1,)), ((), ()))
    s1 = lax.dot_general(keys_ref[0, 0], q[:, :half], nt, preferred_element_type=_f32)
    s2 = lax.dot_general(keys_ref[0, 1], q[:, half:], nt, preferred_element_type=_f32)
    v1, i1 = _topk_rows(s1, PEER_TOPK)
    v2, i2 = _topk_rows(s2, PEER_TOPK)
    cand, cand_ids = _product_candidates(v1, v2)
    vals, pos = _topk_rows(cand, PEER_TOPK, cand_ids)
    pos_hi = jnp.floor(pos * (1.0 / PEER_TOPK))
    pos_lo = pos - pos_hi * PEER_TOPK
    e1 = jnp.zeros_like(pos)
    e2 = jnp.zeros_like(pos)
    for a in range(PEER_TOPK):
        e1 = e1 + jnp.where(pos_hi == a, i1[a:a + 1], 0.0)
        e2 = e2 + jnp.where(pos_lo == a, i2[a:a + 1], 0.0)
    idx_ref[0] = (e1 * n_keys + e2).astype(jnp.int32)
    ex = jnp.exp(vals - vals[0:1])
    gate_ref[0] = ex / jnp.sum(ex, axis=0, keepdims=True)


def _route(xn, w_query_stack, layer, sub_keys):
    t, d = xn.shape
    heads, _, n_keys, half = sub_keys.shape
    qd = 2 * half
    tm = _blk(t, 256)
    out_sds = lambda dt: jax.ShapeDtypeStruct((heads, PEER_TOPK, t), dt)
    idx, gates = pl.pallas_call(
        functools.partial(_route_kernel, n_keys=n_keys),
        grid=(heads, t // tm),
        in_specs=[pl.BlockSpec((tm, d), lambda h, i: (i, 0)),
                  pl.BlockSpec((None, d, qd), lambda h, i: (layer, 0, h)),
                  pl.BlockSpec((1, 2, n_keys, half), lambda h, i: (h, 0, 0, 0))],
        out_specs=[pl.BlockSpec((1, PEER_TOPK, tm), lambda h, i: (h, 0, i)),
                   pl.BlockSpec((1, PEER_TOPK, tm), lambda h, i: (h, 0, i))],
        out_shape=[out_sds(jnp.int32), out_sds(_f32)],
        scratch_shapes=[pltpu.VMEM((d, qd), _bf16)],
        compiler_params=_params("parallel", "arbitrary"),
        name="peer_route",
    )(xn, w_query_stack, sub_keys)
    flat = lambda a: jnp.transpose(a, (2, 0, 1)).reshape(t, heads * PEER_TOPK)
    return flat(idx), flat(gates)


PEER_TOKENS_PER_STEP = 128
PEER_TOKENS_PER_GROUP = 2
PEER_RING = 8
PEER_AHEAD = 6
PEER_ISSUE_UNROLL = 8
BF16_TILE_ROWS = 16
DMA_PRIORITIES = 2


def _peer_kernel(idx_ref, idx_next_ref, gate_ref, x_ref, h_ref, g_ref, tab_ref, *rest, n_sel, emit_h):
    n_out = 2 if emit_h else 1
    outs, scratch = rest[:n_out], rest[n_out:]
    o_ref = outs[0] if emit_h else None
    n_ref = outs[-1]
    bufs, sem_ref, stage_ref = scratch[:PEER_RING], scratch[PEER_RING], scratch[PEER_RING + 1]
    n_rows_ref = n_ref if n_ref.dtype == _f32 else stage_ref
    tb, rows, _ = x_ref.shape
    rr = rows // 2
    grp = PEER_TOKENS_PER_GROUP
    n_groups = tb // grp
    width = n_sel * rr
    step = pl.program_id(0)
    last_step = pl.num_programs(0) - 1

    def widen(a):
        return jnp.concatenate([a[..., 0:rr, :], a[..., rr:2 * rr, :]], axis=-1)

    def slab_matrix(slot, j, first_row):
        part = lambda r0: bufs[slot][pl.ds(j * n_sel, n_sel), r0:r0 + rr, :].reshape(width, LANES)
        return jnp.concatenate([part(first_row), part(first_row + rr)], axis=-1)

    def start_copy(ids_ref, tok, k, slot, j, priority):
        e = ids_ref[tok, k]
        pltpu.make_async_copy(tab_ref.at[e], bufs[slot].at[j * n_sel + k], sem_ref.at[slot]).start(priority=priority)

    def wait_group(slot):
        pltpu.make_async_copy(tab_ref.at[pl.ds(0, grp * n_sel)], bufs[slot], sem_ref.at[slot]).wait()

    def start_group_rolled(ids_ref, group, slot):
        for j in range(grp):
            def issue_chunk(c, carry, j=j):
                for kk in range(PEER_ISSUE_UNROLL):
                    start_copy(ids_ref, group * grp + j, c * PEER_ISSUE_UNROLL + kk, slot, j, kk % DMA_PRIORITIES)
                return carry
            lax.fori_loop(0, n_sel // PEER_ISSUE_UNROLL, issue_chunk, 0)

    lane_r = lax.broadcasted_iota(jnp.int32, (rr, width), 1) % rr
    diag = lane_r == lax.broadcasted_iota(jnp.int32, (rr, width), 0)
    col_k = lax.broadcasted_iota(jnp.int32, (n_sel, width), 1) // rr
    expand = (col_k == lax.broadcasted_iota(jnp.int32, (n_sel, width), 0)).astype(_bf16)
    nt = (((1,), (1,)), ((), ()))

    def score_partial(g, slot, j, prefetch):
        base = g * grp
        if prefetch:
            for k in range(n_sel):
                start_copy(idx_ref, base + PEER_AHEAD * grp + j, k, (slot + PEER_AHEAD) % PEER_RING, j,
                           k % DMA_PRIORITIES)
        u = slab_matrix(slot, j, 0)
        prod = lax.dot_general(widen(x_ref[base + j]), u, nt, preferred_element_type=_f32)
        return jnp.sum(jnp.where(diag, prod, 0.0), axis=0, keepdims=True)

    def group_scores(partial):
        p_hi = partial.astype(_bf16)
        p_lo = (partial - p_hi.astype(_f32)).astype(_bf16)
        both = lax.dot_general(jnp.concatenate([p_hi, p_lo], axis=0), expand, nt, preferred_element_type=_f32)
        return both[:grp] + both[grp:]

    def group_activations(scores, g):
        gate = jnp.concatenate([gate_ref[g * grp + j] for j in range(grp)], axis=0)
        act = jax.nn.gelu(scores) * gate
        return jnp.dot(act.astype(_bf16), expand, preferred_element_type=_f32)

    def group_outputs(act_wide, g, slot):
        ys = []
        for j in range(grp):
            v = slab_matrix(slot, j, 2 * rr)
            lhs = jnp.where(diag, act_wide[j:j + 1, :], 0.0).astype(_bf16)
            out = jnp.dot(lhs, v, preferred_element_type=_f32)
            y = h_ref[g * grp + j] + jnp.concatenate([out[:, :LANES], out[:, LANES:]], axis=0)
            if emit_h:
                store_row(o_ref, g * grp + j, y)
            ys.append(y)
        return tuple(ys)

    def finish_group(partial, g, slot):
        return group_outputs(group_activations(group_scores(partial), g), g, slot)

    def store_row(ref, tok, y):
        ref[pl.ds(tok, 1), :] = pltpu.einshape("erl->e(rl)", y[None])

    def norm_group(ys, g):
        for j, y in enumerate(ys):
            ms = jnp.sum(jnp.sum(y * y, axis=1, keepdims=True), axis=0, keepdims=True) * (1.0 / (rows * LANES))
            store_row(n_rows_ref, g * grp + j, y * lax.rsqrt(ms + RMS_EPS) * g_ref[...])

    def group_step(g, slot, partial_prev, ys_prev2, next_block_group=None):
        wait_group(slot)
        if next_block_group is not None:
            @pl.when(step < last_step)
            def _():
                start_group_rolled(idx_next_ref, next_block_group, (slot + PEER_AHEAD) % PEER_RING)
        prefetch = next_block_group is None
        prev, prev_slot = g - 1, (slot - 1) % PEER_RING
        stages = [] if partial_prev is None else [group_scores, lambda s: group_activations(s, prev)]
        state = partial_prev
        partial = []
        for j in range(grp):
            if j < len(stages):
                state = stages[j](state)
            partial.append(score_partial(g, slot, j, prefetch))
        for stage in stages[grp:]:
            state = stage(state)
        ys = None if partial_prev is None else group_outputs(state, prev, prev_slot)
        if ys_prev2 is not None:
            norm_group(ys_prev2, g - 2)
        return jnp.concatenate(partial, axis=0), ys

    @pl.when(step == 0)
    def _():
        for slot in range(PEER_AHEAD):
            start_group_rolled(idx_ref, slot, slot)

    partial, _ = group_step(0, 0, None, None)
    carry = group_step(1, 1, partial, None)

    def ring_body(it, carry):
        for s in range(2, PEER_RING + 2):
            carry = group_step(it * PEER_RING + s, s % PEER_RING, *carry)
        return carry

    carry = lax.fori_loop(0, n_groups // PEER_RING - 1, ring_body, carry)
    for s in range(2, PEER_RING):
        g = n_groups - PEER_RING + s
        carry = group_step(g, s, *carry, next_block_group=g + PEER_AHEAD - n_groups)
    partial, ys = carry
    norm_group(ys, n_groups - 2)
    norm_group(finish_group(partial, n_groups - 1, PEER_RING - 1), n_groups - 1)
    if n_rows_ref is not n_ref:
        n_ref[...] = stage_ref[...].astype(n_ref.dtype)


def _peer_experts(idx, gates, xn, h, table, norm_g, norm_dtype, emit_h):
    t, d = xn.shape
    rows = d // LANES
    n_sel = idx.shape[1]
    tb = _blk(t, PEER_TOKENS_PER_STEP)
    n_steps = t // tb
    n_groups = tb // PEER_TOKENS_PER_GROUP
    assert n_groups % PEER_RING == 0 and n_groups >= 2 * PEER_RING
    assert n_sel % PEER_ISSUE_UNROLL == 0 and rows % (2 * BF16_TILE_ROWS) == 0
    slot_shape = (PEER_TOKENS_PER_GROUP * n_sel, 2 * rows, LANES)
    row_spec = pl.BlockSpec((tb, rows, LANES), lambda i: (i, 0, 0))
    flat_spec = pl.BlockSpec((tb, d), lambda i: (i, 0))
    out_specs = [flat_spec]
    out_shape = [jax.ShapeDtypeStruct((t, d), norm_dtype)]
    if emit_h:
        out_specs.insert(0, flat_spec)
        out_shape.insert(0, jax.ShapeDtypeStruct((t, d), _f32))
    outs = pl.pallas_call(
        functools.partial(_peer_kernel, n_sel=n_sel, emit_h=emit_h),
        grid=(n_steps,),
        in_specs=[
            pl.BlockSpec((tb, n_sel), lambda i: (i, 0), memory_space=pltpu.SMEM),
            pl.BlockSpec((tb, n_sel), lambda i: (jnp.minimum(i + 1, n_steps - 1), 0), memory_space=pltpu.SMEM),
            pl.BlockSpec((tb, 1, n_sel), lambda i: (i, 0, 0)),
            row_spec,
            row_spec,
            pl.BlockSpec((rows, LANES), lambda i: (0, 0)),
            pl.BlockSpec(memory_space=pl.ANY),
        ],
        out_specs=out_specs,
        out_shape=out_shape,
        scratch_shapes=[pltpu.VMEM(slot_shape, _bf16) for _ in range(PEER_RING)]
                       + [pltpu.SemaphoreType.DMA((PEER_RING,)), pltpu.VMEM((tb, d), _f32)],
        compiler_params=_params("arbitrary"),
        name="peer_experts",
    )(idx, idx, gates.reshape(t, 1, n_sel), xn.reshape(t, rows, LANES), h.reshape(t, rows, LANES),
      norm_g.reshape(rows, LANES), table)
    return (outs[0] if emit_h else None), outs[-1]


TABLE_EXPERTS_PER_STEP = 256


def _table_kernel(u_ref, v_ref, o_ref):
    rows = o_ref.shape[1] // 2
    for src, first in ((u_ref, 0), (v_ref, rows)):
        slab = pltpu.einshape("e(rl)->erl", src[...], l=LANES)
        o_ref[:, first:first + rows, :] = slab.astype(o_ref.dtype)


def _expert_table(u_stack, v_stack, layer):
    _, e, d = u_stack.shape
    rows = d // LANES
    eb = _blk(e, TABLE_EXPERTS_PER_STEP)
    src_spec = pl.BlockSpec((None, eb, d), lambda i: (layer, i, 0))
    return pl.pallas_call(
        _table_kernel,
        grid=(e // eb,),
        in_specs=[src_spec, src_spec],
        out_specs=pl.BlockSpec((eb, 2 * rows, LANES), lambda i: (i, 0, 0)),
        out_shape=jax.ShapeDtypeStruct((e, 2 * rows, LANES), _bf16),
        compiler_params=_params("parallel"),
        name="expert_table",
    )(u_stack, v_stack)


def kernel(x, mix_norm, w_in, conv_w, conv_b, w_rg, b_rg, w_ig, b_ig, lru_lambda, w_pool, pool_scale,
           w_out, ffn_norm, w_query, sub_keys, expert_u, expert_v, final_norm):
    batch, seq, d = x.shape
    depth = w_in.shape[0]
    t = batch * seq
    h = x.reshape(t, d)
    xn = _rmsnorm(h, mix_norm[0], _bf16)
    for l in range(depth):
        last = l == depth - 1
        z = _matmul(xn, w_in, l, name="in_proj")
        ypool = _pool_branch(z, w_pool[l].astype(_bf16), pool_scale[l], batch, seq, d)
        merged = _lru_branch(z, ypool, conv_w[l], conv_b[l], w_rg[l].astype(_bf16), b_rg[l],
                             w_ig[l].astype(_bf16), b_ig[l], lru_lambda[l], batch, seq, d)
        h = _matmul(merged, w_out, l, residual=h, name="out_proj")
        xn = _rmsnorm(h, ffn_norm[l], _bf16)
        idx, gates = _route(xn, w_query, l, sub_keys[l].astype(_bf16))
        h, xn = _peer_experts(idx, gates, xn, h, _expert_table(expert_u, expert_v, l),
                              final_norm if last else mix_norm[l + 1], _f32 if last else _bf16, emit_h=not last)
    return xn.reshape(batch, seq, d)
```

```python
import functools

import jax
import jax.numpy as jnp
from jax import lax
from jax.experimental import pallas as pl
from jax.experimental.pallas import tpu as pltpu

RMS_EPS = 1e-6
RG_C = 8.0
CONV_WIDTH = 4
POOL_WINDOWS = (2, 4, 8, 16)
PEER_TOPK = 16

LANES = 128
SUBLANES = 8
VMEM_LIMIT_BYTES = 56 * 1024 * 1024
HALO_ROWS_POOL = 16
HALO_ROWS_CONV = 8
POOL_TIME_BLOCK = 256
LRU_TIME_BLOCK = 1024

_f32 = jnp.float32
_bf16 = jnp.bfloat16


def _params(*sem):
    return pltpu.CompilerParams(dimension_semantics=sem, vmem_limit_bytes=VMEM_LIMIT_BYTES)


def _blk(dim, pref):
    return pref if dim % pref == 0 else dim


def _rmsnorm_kernel(x_ref, g_ref, o_ref):
    x = x_ref[...]
    ms = jnp.mean(x * x, axis=-1, keepdims=True)
    o_ref[...] = (x * lax.rsqrt(ms + RMS_EPS) * g_ref[...]).astype(o_ref.dtype)


def _rmsnorm(x, g, out_dtype):
    t, d = x.shape
    tm = _blk(t, 256)
    return pl.pallas_call(
        _rmsnorm_kernel,
        grid=(t // tm,),
        in_specs=[pl.BlockSpec((tm, d), lambda i: (i, 0)),
                  pl.BlockSpec((1, d), lambda i: (0, 0))],
        out_specs=pl.BlockSpec((tm, d), lambda i: (i, 0)),
        out_shape=jax.ShapeDtypeStruct((t, d), out_dtype),
        compiler_params=_params("parallel"),
        name="rmsnorm",
    )(x, g.reshape(1, d))


def _matmul_kernel(a_ref, w_ref, o_ref):
    w = w_ref[...].astype(_bf16)
    o_ref[...] = jnp.dot(a_ref[...], w, preferred_element_type=_f32).astype(o_ref.dtype)


def _matmul_res_kernel(a_ref, w_ref, r_ref, o_ref):
    w = w_ref[...].astype(_bf16)
    o_ref[...] = r_ref[...] + jnp.dot(a_ref[...], w, preferred_element_type=_f32)


def _matmul(a, w_stack, layer, residual=None, out_dtype=_f32, name="matmul"):
    m, k = a.shape
    n = w_stack.shape[2]
    tm = _blk(m, 1024)
    tn = _blk(n, 512)
    in_specs = [pl.BlockSpec((tm, k), lambda i, j: (i, 0)),
                pl.BlockSpec((None, k, tn), lambda i, j: (layer, 0, j))]
    w = w_stack
    args = [a, w]
    kern = _matmul_kernel
    if residual is not None:
        in_specs.append(pl.BlockSpec((tm, tn), lambda i, j: (i, j)))
        args.append(residual)
        kern = _matmul_res_kernel
    return pl.pallas_call(
        kern,
        grid=(m // tm, n // tn),
        in_specs=in_specs,
        out_specs=pl.BlockSpec((tm, tn), lambda i, j: (i, j)),
        out_shape=jax.ShapeDtypeStruct((m, n), out_dtype),
        compiler_params=_params("parallel", "arbitrary"),
        name=name,
    )(*args)


def _pool_kernel(x_ref, halo_ref, gate_ref, w_ref, scale_ref, o_ref, *, group_dim):
    t = pl.program_id(1)
    ts = x_ref.shape[0]
    row = lax.broadcasted_iota(jnp.int32, (ts, 1), 0) + t * ts
    first = t == 0
    for g, win in enumerate(POOL_WINDOWS):
        cols = slice(g * group_dim, (g + 1) * group_dim)
        x = x_ref[:, cols]
        halo = jnp.where(first, 0.0, halo_ref[:, cols])
        acc = jnp.concatenate([halo, x], axis=0)
        step = 1
        while step < win:
            acc = acc + pltpu.roll(acc, step, axis=0)
            step *= 2
        wsum = acc[HALO_ROWS_POOL:]
        count = jnp.minimum(row + 1, win).astype(_f32)
        pooled = wsum / count - x
        y = jnp.dot(pooled.astype(_bf16), w_ref[g], preferred_element_type=_f32)
        y = y * scale_ref[:, cols]
        o_ref[:, cols] = (jax.nn.sigmoid(gate_ref[:, cols]) * y).astype(o_ref.dtype)


def _pool_branch(z, w_pool, pool_scale, batch, seq, d):
    t = batch * seq
    groups, group_dim = w_pool.shape[0], w_pool.shape[1]
    ts = _blk(seq, POOL_TIME_BLOCK)
    nt = seq // ts
    hb = ts // HALO_ROWS_POOL
    x_col, gate_col = 2, 4
    return pl.pallas_call(
        functools.partial(_pool_kernel, group_dim=group_dim),
        grid=(batch, nt),
        in_specs=[
            pl.BlockSpec((ts, d), lambda b, s: (b * nt + s, x_col)),
            pl.BlockSpec((HALO_ROWS_POOL, d),
                         lambda b, s: (jnp.maximum((b * nt + s) * hb - 1, 0), x_col)),
            pl.BlockSpec((ts, d), lambda b, s: (b * nt + s, gate_col)),
            pl.BlockSpec((groups, group_dim, group_dim), lambda b, s: (0, 0, 0)),
            pl.BlockSpec((1, d), lambda b, s: (0, 0)),
        ],
        out_specs=pl.BlockSpec((ts, d), lambda b, s: (b * nt + s, 0)),
        out_shape=jax.ShapeDtypeStruct((t, d), _bf16),
        compiler_params=_params("parallel", "arbitrary"),
        name="pool_branch",
    )(z, z, z, w_pool, pool_scale.reshape(1, d))


def _lru_kernel(x_ref, halo_ref, gelu_ref, glru_ref, ypool_ref, convw_ref, convb_ref,
                wrg_ref, brg_ref, wig_ref, big_ref, lam_ref, o_ref, carry_ref):
    t = pl.program_id(2)
    ts = x_ref.shape[0]

    @pl.when(t == 0)
    def _():
        carry_ref[...] = jnp.zeros_like(carry_ref)

    x = x_ref[...]
    halo = jnp.where(t == 0, 0.0, halo_ref[...])
    ext = jnp.concatenate([halo, x], axis=0)
    conv = convb_ref[...] + x * convw_ref[CONV_WIDTH - 1:CONV_WIDTH, :]
    for k in range(CONV_WIDTH - 1):
        shifted = pltpu.roll(ext, CONV_WIDTH - 1 - k, axis=0)[HALO_ROWS_CONV:]
        conv = conv + shifted * convw_ref[k:k + 1, :]

    cb = conv.astype(_bf16)
    r = jax.nn.sigmoid(jnp.dot(cb, wrg_ref[0], preferred_element_type=_f32) + brg_ref[...])
    i = jax.nn.sigmoid(jnp.dot(cb, wig_ref[0], preferred_element_type=_f32) + big_ref[...])
    neg_lam = -lam_ref[...]
    softplus = jnp.maximum(neg_lam, 0.0) + jnp.log(1.0 + jnp.exp(-jnp.abs(neg_lam)))
    log_a = -RG_C * r * softplus
    a = jnp.exp(log_a)
    u = jnp.sqrt(1.0 - a * a) * (i * conv)

    tiles = ts // SUBLANES
    a = a.reshape(tiles, SUBLANES, a.shape[-1])
    u = u.reshape(tiles, SUBLANES, u.shape[-1])
    sub = lax.broadcasted_iota(jnp.int32, a.shape, 1)
    step = 1
    while step < SUBLANES:
        keep = sub >= step
        a_prev = jnp.where(keep, pltpu.roll(a, step, axis=1), 1.0)
        u_prev = jnp.where(keep, pltpu.roll(u, step, axis=1), 0.0)
        u = a * u_prev + u
        a = a * a_prev
        step *= 2
    carry = carry_ref[...]
    hs = []
    for i in range(tiles):
        hs.append(a[i] * carry + u[i])
        carry = hs[-1][SUBLANES - 1:SUBLANES, :]
    carry_ref[...] = carry
    h = jnp.concatenate(hs, axis=0)

    y = h * jax.nn.gelu(gelu_ref[...])
    merged = jax.nn.sigmoid(glru_ref[...]) * y + ypool_ref[...].astype(_f32)
    o_ref[...] = merged.astype(o_ref.dtype)


def _lru_branch(z, ypool, conv_w, conv_b, w_rg, b_rg, w_ig, b_ig, lam, batch, seq, d):
    t = batch * seq
    heads, hd = w_rg.shape[0], w_rg.shape[1]
    ts = _blk(seq, LRU_TIME_BLOCK)
    nt = seq // ts
    hb = ts // HALO_ROWS_CONV
    nh = d // hd
    row = lambda b, h, s: b * nt + s
    vec = pl.BlockSpec((1, hd), lambda b, h, s: (0, h))
    return pl.pallas_call(
        _lru_kernel,
        grid=(batch, heads, nt),
        in_specs=[
            pl.BlockSpec((ts, hd), lambda b, h, s: (row(b, h, s), h)),
            pl.BlockSpec((HALO_ROWS_CONV, hd),
                         lambda b, h, s: (jnp.maximum(row(b, h, s) * hb - 1, 0), h)),
            pl.BlockSpec((ts, hd), lambda b, h, s: (row(b, h, s), nh + h)),
            pl.BlockSpec((ts, hd), lambda b, h, s: (row(b, h, s), 3 * nh + h)),
            pl.BlockSpec((ts, hd), lambda b, h, s: (row(b, h, s), h)),
            pl.BlockSpec((CONV_WIDTH, hd), lambda b, h, s: (0, h)),
            vec,
            pl.BlockSpec((1, hd, hd), lambda b, h, s: (h, 0, 0)),
            vec,
            pl.BlockSpec((1, hd, hd), lambda b, h, s: (h, 0, 0)),
            vec,
            vec,
        ],
        out_specs=pl.BlockSpec((ts, hd), lambda b, h, s: (row(b, h, s), h)),
        out_shape=jax.ShapeDtypeStruct((t, d), _bf16),
        scratch_shapes=[pltpu.VMEM((1, hd), _f32)],
        compiler_params=_params("parallel", "parallel", "arbitrary"),
        name="lru_branch",
    )(z, z, z, z, ypool, conv_w, conv_b.reshape(1, d), w_rg, b_rg.reshape(1, d),
      w_ig, b_ig.reshape(1, d), lam.reshape(1, d))


def _topk_rows(s, k, ids=None):
    if ids is None:
        ids = lax.broadcasted_iota(jnp.int32, s.shape, 0).astype(_f32)
    big = float(2 ** 20)
    vals, idxs = [], []
    for _ in range(k):
        m = jnp.max(s, axis=0, keepdims=True)
        idx = jnp.min(jnp.where(s == m, ids, big), axis=0, keepdims=True)
        vals.append(m)
        idxs.append(idx)
        s = jnp.where(ids == idx, -jnp.inf, s)
    return jnp.concatenate(vals, axis=0), jnp.concatenate(idxs, axis=0)


def _product_candidates(v1, v2):
    k = PEER_TOPK
    half = k // 2
    n = v1.shape[1]
    sub = lax.broadcasted_iota(jnp.int32, (half, n), 0)
    blocks = [v1[0:1] + v2]
    ids = [lax.broadcasted_iota(jnp.int32, (k, n), 0)]
    blocks.append(v1[1:2] + v2[0:half])
    ids.append(k + sub)
    blocks.append(v1[half:k] + v2[0:1])
    ids.append((sub + half) * k)
    for b in range(k // 3):
        ok = (sub >= 2) & ((sub + 1) * (b + 1) <= k)
        blocks.append(jnp.where(ok, v1[0:half] + v2[b:b + 1], -jnp.inf))
        ids.append(sub * k + b)
    return jnp.concatenate(blocks, axis=0), jnp.concatenate(ids, axis=0).astype(_f32)


def _route_kernel(x_ref, wq_ref, keys_ref, idx_ref, gate_ref, wq_bf16_ref, *, n_keys):
    half = keys_ref.shape[-1]
    @pl.when(pl.program_id(1) == 0)
    def _():
        wq_bf16_ref[...] = wq_ref[...].astype(_bf16)

    q = jnp.dot(x_ref[...], wq_bf16_ref[...], preferred_element_type=_f32).astype(_bf16)
    nt = (((1,), (1,)), ((), ()))
    s1 = lax.dot_general(keys_ref[0, 0], q[:, :half], nt, preferred_element_type=_f32)
    s2 = lax.dot_general(keys_ref[0, 1], q[:, half:], nt, preferred_element_type=_f32)
    v1, i1 = _topk_rows(s1, PEER_TOPK)
    v2, i2 = _topk_rows(s2, PEER_TOPK)
    cand, cand_ids = _product_candidates(v1, v2)
    vals, pos = _topk_rows(cand, PEER_TOPK, cand_ids)
    pos_hi = jnp.floor(pos * (1.0 / PEER_TOPK))
    pos_lo = pos - pos_hi * PEER_TOPK
    e1 = jnp.zeros_like(pos)
    e2 = jnp.zeros_like(pos)
    for a in range(PEER_TOPK):
        e1 = e1 + jnp.where(pos_hi == a, i1[a:a + 1], 0.0)
        e2 = e2 + jnp.where(pos_lo == a, i2[a:a + 1], 0.0)
    idx_ref[0] = (e1 * n_keys + e2).astype(jnp.int32)
    ex = jnp.exp(vals - vals[0:1])
    gate_ref[0] = ex / jnp.sum(ex, axis=0, keepdims=True)


def _route(xn, w_query_stack, layer, sub_keys):
    t, d = xn.shape
    heads, _, n_keys, half = sub_keys.shape
    qd = 2 * half
    tm = _blk(t, 512)
    out_sds = lambda dt: jax.ShapeDtypeStruct((heads, PEER_TOPK, t), dt)
    idx, gates = pl.pallas_call(
        functools.partial(_route_kernel, n_keys=n_keys),
        grid=(heads, t // tm),
        in_specs=[pl.BlockSpec((tm, d), lambda h, i: (i, 0)),
                  pl.BlockSpec((None, d, qd), lambda h, i: (layer, 0, h)),
                  pl.BlockSpec((1, 2, n_keys, half), lambda h, i: (h, 0, 0, 0))],
        out_specs=[pl.BlockSpec((1, PEER_TOPK, tm), lambda h, i: (h, 0, i)),
                   pl.BlockSpec((1, PEER_TOPK, tm), lambda h, i: (h, 0, i))],
        out_shape=[out_sds(jnp.int32), out_sds(_f32)],
        scratch_shapes=[pltpu.VMEM((d, qd), _bf16)],
        compiler_params=_params("parallel", "arbitrary"),
        name="peer_route",
    )(xn, w_query_stack, sub_keys)
    flat = lambda a: jnp.transpose(a, (2, 0, 1)).reshape(t, heads * PEER_TOPK)
    return flat(idx), flat(gates)


PEER_TOKENS_PER_STEP = 128
PEER_TOKENS_PER_GROUP = 2
PEER_RING = 8
PEER_AHEAD = 6
PEER_ISSUE_UNROLL = 8
BF16_TILE_ROWS = 16
DMA_PRIORITIES = 2


def _peer_kernel(idx_ref, idx_next_ref, gate_ref, x_ref, h_ref, g_ref, tab_ref, *rest, n_sel, emit_h):
    n_out = 2 if emit_h else 1
    outs, scratch = rest[:n_out], rest[n_out:]
    o_ref = outs[0] if emit_h else None
    n_ref = outs[-1]
    bufs, sem_ref, stage_ref = scratch[:PEER_RING], scratch[PEER_RING], scratch[PEER_RING + 1]
    n_rows_ref = n_ref if n_ref.dtype == _f32 else stage_ref
    tb, rows, _ = x_ref.shape
    rr = rows // 2
    grp = PEER_TOKENS_PER_GROUP
    n_groups = tb // grp
    width = n_sel * rr
    step = pl.program_id(0)
    last_step = pl.num_programs(0) - 1

    def widen(a):
        return jnp.concatenate([a[..., 0:rr, :], a[..., rr:2 * rr, :]], axis=-1)

    def slab_matrix(slot, j, first_row):
        part = lambda r0: bufs[slot][pl.ds(j * n_sel, n_sel), r0:r0 + rr, :].reshape(width, LANES)
        return jnp.concatenate([part(first_row), part(first_row + rr)], axis=-1)

    def start_copy(ids_ref, tok, k, slot, j, priority):
        e = ids_ref[tok, k]
        pltpu.make_async_copy(tab_ref.at[e], bufs[slot].at[j * n_sel + k], sem_ref.at[slot]).start(priority=priority)

    def wait_group(slot):
        pltpu.make_async_copy(tab_ref.at[pl.ds(0, grp * n_sel)], bufs[slot], sem_ref.at[slot]).wait()

    def start_group_rolled(ids_ref, group, slot):
        for j in range(grp):
            def issue_chunk(c, carry, j=j):
                for kk in range(PEER_ISSUE_UNROLL):
                    start_copy(ids_ref, group * grp + j, c * PEER_ISSUE_UNROLL + kk, slot, j, kk % DMA_PRIORITIES)
                return carry
            lax.fori_loop(0, n_sel // PEER_ISSUE_UNROLL, issue_chunk, 0)

    lane_r = lax.broadcasted_iota(jnp.int32, (rr, width), 1) % rr
    diag = lane_r == lax.broadcasted_iota(jnp.int32, (rr, width), 0)
    col_k = lax.broadcasted_iota(jnp.int32, (n_sel, width), 1) // rr
    expand = (col_k == lax.broadcasted_iota(jnp.int32, (n_sel, width), 0)).astype(_bf16)
    nt = (((1,), (1,)), ((), ()))

    def score_partial(g, slot, j, prefetch):
        base = g * grp
        if prefetch:
            for k in range(n_sel):
                start_copy(idx_ref, base + PEER_AHEAD * grp + j, k, (slot + PEER_AHEAD) % PEER_RING, j,
                           k % DMA_PRIORITIES)
        u = slab_matrix(slot, j, 0)
        prod = lax.dot_general(widen(x_ref[base + j]), u, nt, preferred_element_type=_f32)
        return jnp.sum(jnp.where(diag, prod, 0.0), axis=0, keepdims=True)

    def group_scores(partial):
        p_hi = partial.astype(_bf16)
        p_lo = (partial - p_hi.astype(_f32)).astype(_bf16)
        both = lax.dot_general(jnp.concatenate([p_hi, p_lo], axis=0), expand, nt, preferred_element_type=_f32)
        return both[:grp] + both[grp:]

    def group_activations(scores, g):
        gate = jnp.concatenate([gate_ref[g * grp + j] for j in range(grp)], axis=0)
        act = jax.nn.gelu(scores) * gate
        return jnp.dot(act.astype(_bf16), expand, preferred_element_type=_f32)

    def group_outputs(act_wide, g, slot):
        ys = []
        for j in range(grp):
            v = slab_matrix(slot, j, 2 * rr)
            lhs = jnp.where(diag, act_wide[j:j + 1, :], 0.0).astype(_bf16)
            out = jnp.dot(lhs, v, preferred_element_type=_f32)
            y = h_ref[g * grp + j] + jnp.concatenate([out[:, :LANES], out[:, LANES:]], axis=0)
            if emit_h:
                store_row(o_ref, g * grp + j, y)
            ys.append(y)
        return tuple(ys)

    def finish_group(partial, g, slot):
        return group_outputs(group_activations(group_scores(partial), g), g, slot)

    def store_row(ref, tok, y):
        ref[pl.ds(tok, 1), :] = pltpu.einshape("erl->e(rl)", y[None])

    def norm_group(ys, g):
        for j, y in enumerate(ys):
            ms = jnp.sum(jnp.sum(y * y, axis=1, keepdims=True), axis=0, keepdims=True) * (1.0 / (rows * LANES))
            store_row(n_rows_ref, g * grp + j, y * lax.rsqrt(ms + RMS_EPS) * g_ref[...])

    def group_step(g, slot, partial_prev, ys_prev2, next_block_group=None):
        wait_group(slot)
        if next_block_group is not None:
            @pl.when(step < last_step)
            def _():
                start_group_rolled(idx_next_ref, next_block_group, (slot + PEER_AHEAD) % PEER_RING)
        prefetch = next_block_group is None
        prev, prev_slot = g - 1, (slot - 1) % PEER_RING
        stages = [] if partial_prev is None else [group_scores, lambda s: group_activations(s, prev)]
        state = partial_prev
        partial = []
        for j in range(grp):
            if j < len(stages):
                state = stages[j](state)
            partial.append(score_partial(g, slot, j, prefetch))
        for stage in stages[grp:]:
            state = stage(state)
        ys = None if partial_prev is None else group_outputs(state, prev, prev_slot)
        if ys_prev2 is not None:
            norm_group(ys_prev2, g - 2)
        return jnp.concatenate(partial, axis=0), ys

    @pl.when(step == 0)
    def _():
        for slot in range(PEER_AHEAD):
            start_group_rolled(idx_ref, slot, slot)

    partial, _ = group_step(0, 0, None, None)
    carry = group_step(1, 1, partial, None)

    def ring_body(it, carry):
        for s in range(2, PEER_RING + 2):
            carry = group_step(it * PEER_RING + s, s % PEER_RING, *carry)
        return carry

    carry = lax.fori_loop(0, n_groups // PEER_RING - 1, ring_body, carry)
    for s in range(2, PEER_RING):
        g = n_groups - PEER_RING + s
        carry = group_step(g, s, *carry, next_block_group=g + PEER_AHEAD - n_groups)
    partial, ys = carry
    norm_group(ys, n_groups - 2)
    norm_group(finish_group(partial, n_groups - 1, PEER_RING - 1), n_groups - 1)
    if n_rows_ref is not n_ref:
        n_ref[...] = stage_ref[...].astype(n_ref.dtype)


def _peer_experts(idx, gates, xn, h, table, norm_g, norm_dtype, emit_h):
    t, d = xn.shape
    rows = d // LANES
    n_sel = idx.shape[1]
    tb = _blk(t, PEER_TOKENS_PER_STEP)
    n_steps = t // tb
    n_groups = tb // PEER_TOKENS_PER_GROUP
    assert n_groups % PEER_RING == 0 and n_groups >= 2 * PEER_RING
    assert n_sel % PEER_ISSUE_UNROLL == 0 and rows % (2 * BF16_TILE_ROWS) == 0
    slot_shape = (PEER_TOKENS_PER_GROUP * n_sel, 2 * rows, LANES)
    row_spec = pl.BlockSpec((tb, rows, LANES), lambda i: (i, 0, 0))
    flat_spec = pl.BlockSpec((tb, d), lambda i: (i, 0))
    out_specs = [flat_spec]
    out_shape = [jax.ShapeDtypeStruct((t, d), norm_dtype)]
    if emit_h:
        out_specs.insert(0, flat_spec)
        out_shape.insert(0, jax.ShapeDtypeStruct((t, d), _f32))
    outs = pl.pallas_call(
        functools.partial(_peer_kernel, n_sel=n_sel, emit_h=emit_h),
        grid=(n_steps,),
        in_specs=[
            pl.BlockSpec((tb, n_sel), lambda i: (i, 0), memory_space=pltpu.SMEM),
            pl.BlockSpec((tb, n_sel), lambda i: (jnp.minimum(i + 1, n_steps - 1), 0), memory_space=pltpu.SMEM),
            pl.BlockSpec((tb, 1, n_sel), lambda i: (i, 0, 0)),
            row_spec,
            row_spec,
            pl.BlockSpec((rows, LANES), lambda i: (0, 0)),
            pl.BlockSpec(memory_space=pl.ANY),
        ],
        out_specs=out_specs,
        out_shape=out_shape,
        scratch_shapes=[pltpu.VMEM(slot_shape, _bf16) for _ in range(PEER_RING)]
                       + [pltpu.SemaphoreType.DMA((PEER_RING,)), pltpu.VMEM((tb, d), _f32)],
        compiler_params=_params("arbitrary"),
        name="peer_experts",
    )(idx, idx, gates.reshape(t, 1, n_sel), xn.reshape(t, rows, LANES), h.reshape(t, rows, LANES),
      norm_g.reshape(rows, LANES), table)
    return (outs[0] if emit_h else None), outs[-1]


TABLE_EXPERTS_PER_STEP = 256


def _table_kernel(u_ref, v_ref, o_ref):
    rows = o_ref.shape[1] // 2
    for src, first in ((u_ref, 0), (v_ref, rows)):
        slab = pltpu.einshape("e(rl)->erl", src[...], l=LANES)
        o_ref[:, first:first + rows, :] = slab.astype(o_ref.dtype)


def _expert_table(u_stack, v_stack, layer):
    _, e, d = u_stack.shape
    rows = d // LANES
    eb = _blk(e, TABLE_EXPERTS_PER_STEP)
    src_spec = pl.BlockSpec((None, eb, d), lambda i: (layer, i, 0))
    return pl.pallas_call(
        _table_kernel,
        grid=(e // eb,),
        in_specs=[src_spec, src_spec],
        out_specs=pl.BlockSpec((eb, 2 * rows, LANES), lambda i: (i, 0, 0)),
        out_shape=jax.ShapeDtypeStruct((e, 2 * rows, LANES), _bf16),
        compiler_params=_params("parallel"),
        name="expert_table",
    )(u_stack, v_stack)


def kernel(x, mix_norm, w_in, conv_w, conv_b, w_rg, b_rg, w_ig, b_ig, lru_lambda, w_pool, pool_scale,
           w_out, ffn_norm, w_query, sub_keys, expert_u, expert_v, final_norm):
    batch, seq, d = x.shape
    depth = w_in.shape[0]
    t = batch * seq
    h = x.reshape(t, d)
    xn = _rmsnorm(h, mix_norm[0], _bf16)
    for l in range(depth):
        last = l == depth - 1
        z = _matmul(xn, w_in, l, name="in_proj")
        ypool = _pool_branch(z, w_pool[l].astype(_bf16), pool_scale[l], batch, seq, d)
        merged = _lru_branch(z, ypool, conv_w[l], conv_b[l], w_rg[l].astype(_bf16), b_rg[l],
                             w_ig[l].astype(_bf16), b_ig[l], lru_lambda[l], batch, seq, d)
        h = _matmul(merged, w_out, l, residual=h, name="out_proj")
        xn = _rmsnorm(h, ffn_norm[l], _bf16)
        idx, gates = _route(xn, w_query, l, sub_keys[l].astype(_bf16))
        h, xn = _peer_experts(idx, gates, xn, h, _expert_table(expert_u, expert_v, l),
                              final_norm if last else mix_norm[l + 1], _f32 if last else _bf16, emit_h=not last)
    return xn.reshape(batch, seq, d)
```
